```python
import jax, jax.numpy as jnp
from jax import lax
import numpy as np

D_MODEL = 1024
BATCH = 4
SEQ = 4096
DEPTH = 4
DEC_BATCH = 128
DEC_SEQ = 8
PAST_LEN = 2048
PAGE_SIZE = 128

H_A = 8
D_HA = 64
W_A = H_A * D_HA
Q_BLOCK = 128
SB_BIAS_INIT = -7.0
W_B = 512
CONV_W = 3
H_C = 4
DK_C = 128
DV_C = 128
W_C = H_C * DK_C
W_CV = H_C * DV_C
HG_CHUNK = 64
D_FF = 2816
ALPHA = (2 * DEPTH) ** 0.25
BETA = (8 * DEPTH) ** -0.25
LN_EPS = 1e-5
RMS_EPS = 1e-6
IN_SIZES = (W_A, W_A, W_A, W_B, W_B, W_B, W_C, W_C, W_CV, W_CV, D_MODEL, D_MODEL, D_MODEL)
D_IN = sum(IN_SIZES)

kernel_name = 'stickbreak_shortconv_hgrn2_convffn_step'


def layer_norm(x, g, b):
    xf = x.astype(jnp.float32)
    xc = xf - xf.mean(-1, keepdims=True)
    var = jnp.mean(xc * xc, -1, keepdims=True)
    return (xc * lax.rsqrt(var + LN_EPS) * g + b).astype(x.dtype)


def rms_norm(x, g):
    xf = x.astype(jnp.float32)
    return xf * lax.rsqrt(jnp.mean(xf * xf, -1, keepdims=True) + RMS_EPS) * g


def causal_dwconv(u, w, buf):
    T = u.shape[1]
    full = jnp.concatenate([buf.astype(u.dtype), u], axis=1)
    y = sum(w[j] * full[:, j:j + T] for j in range(CONV_W))
    return y, full[:, T:]


def sb_block(q, q_pos, k, v, k_pos, bias):
    z = jnp.einsum('bqhd,bkhd->bhqk', q, k).astype(jnp.float32) * (q.shape[-1] ** -0.5)
    z = z + bias.astype(jnp.float32)[None, :, None, None]
    causal = k_pos[None, :] < q_pos[:, None]
    log_beta = jax.nn.log_sigmoid(z)
    log_keep = jnp.where(causal, log_beta - z, 0.0)
    later = lax.cumsum(log_keep, axis=3, reverse=True) - log_keep
    w = jnp.where(causal, jnp.exp(log_beta + later), 0.0)
    return jnp.einsum('bhqk,bkhd->bqhd', w.astype(v.dtype), v)


def sb_prompt(q, k, v, bias):
    B, T, H, D = q.shape
    nb = T // Q_BLOCK
    pos = jnp.arange(T)
    qb = q.reshape(B, nb, Q_BLOCK, H, D).transpose(1, 0, 2, 3, 4)
    pb = pos.reshape(nb, Q_BLOCK)
    out = lax.map(lambda a: sb_block(a[0], a[1], k, v, pos, bias), (qb, pb))
    return out.transpose(1, 0, 2, 3, 4).reshape(B, T, H, D)


def hgrn_gates(z, lb):
    zf = z.astype(jnp.float32)
    lb = lb.astype(jnp.float32)
    logf = jnp.logaddexp(jnp.log(lb), jnp.log1p(-lb) + jax.nn.log_sigmoid(zf))
    k = (1.0 - lb) * jax.nn.sigmoid(-zf)
    return logf, k


def hgrn2(q, k, v, logf, S0):
    B, T, H, DK = q.shape
    DV = v.shape[-1]
    c = min(HG_CHUNK, T)
    n = -(-T // c)
    pad = n * c - T

    def blocks(a):
        a = jnp.pad(a.astype(jnp.float32), ((0, 0), (0, pad), (0, 0), (0, 0)))
        return a.reshape(B, n, c, H, a.shape[-1]).transpose(1, 0, 2, 3, 4)

    tril = jnp.tril(jnp.ones((c, c), bool))[None, :, :, None, None]

    def step(S, inp):
        qc, kc, vc, gc = inp
        b = jnp.cumsum(gc, axis=1)
        diff = b[:, :, None] - b[:, None, :]
        decay = jnp.where(tril, jnp.exp(jnp.where(tril, diff, 0.0)), 0.0)
        scores = jnp.einsum('bthk,btshk,bshk->bhts', qc, decay, kc)
        o = (jnp.einsum('bhts,bshv->bthv', scores, vc)
             + jnp.einsum('bthk,bhkv->bthv', qc * jnp.exp(b), S))
        b_last = b[:, -1]
        S = (jnp.exp(b_last)[..., None] * S
             + jnp.einsum('bshk,bshv->bhkv', kc * jnp.exp(b_last[:, None] - b), vc))
        return S, o

    S, o = lax.scan(step, S0.astype(jnp.float32), (blocks(q), blocks(k), blocks(v), blocks(logf)))
    o = o.transpose(1, 0, 2, 3, 4).reshape(B, n * c, H, DV)[:, :T]
    return o, S


def trunk_layer(x, attend, conv_buf, S0, ffn_buf, lb, w_in, sb_bias, conv_w, norm_w, w_br_a, w_br_b,
                w_br_c, w_o, ln1_g, ln1_b, w_up, ffn_conv_w, ffn_conv_b, w_down, ln2_g, ln2_b):
    B, T, _ = x.shape
    splits = [int(i) for i in np.cumsum(IN_SIZES)[:-1]]
    (q_a, k_a, v_a, gate_b, gate_c, h_b, q_c, f_c, i_c, og_c,
     m_a, m_b, m_c) = jnp.split(x @ w_in, splits, axis=-1)

    def heads(a, h):
        return a.reshape(B, T, h, -1)

    k_a = heads(k_a, H_A)
    v_a = heads(v_a, H_A)
    y_a = attend(heads(q_a, H_A), k_a, v_a, sb_bias).reshape(B, T, W_A)
    conv_out, conv_new = causal_dwconv(gate_c * h_b, conv_w, conv_buf)
    y_b = gate_b * conv_out
    logf, k_c = hgrn_gates(f_c, lb)
    o_c, S_new = hgrn2(heads(q_c, H_C), heads(k_c, H_C), heads(i_c, H_C), heads(logf, H_C), S0)
    o_c = rms_norm(o_c, norm_w) * jax.nn.silu(heads(og_c, H_C).astype(jnp.float32))
    y_c = o_c.reshape(B, T, W_CV).astype(x.dtype)
    mix = (jax.nn.sigmoid(m_a) * (y_a @ w_br_a)
           + jax.nn.sigmoid(m_b) * (y_b @ w_br_b)
           + jax.nn.sigmoid(m_c) * (y_c @ w_br_c))
    x = layer_norm(ALPHA * x + mix @ w_o, ln1_g, ln1_b)
    a, g = jnp.split(x @ w_up, 2, axis=-1)
    a_conv, ffn_new = causal_dwconv(a, ffn_conv_w, ffn_buf)
    h = jax.nn.gelu(a_conv + ffn_conv_b) * g
    x = layer_norm(ALPHA * x + h @ w_down, ln2_g, ln2_b)
    return x, k_a, v_a, conv_new, S_new.astype(S0.dtype), ffn_new


def setup_inputs(seed: int = 0) -> dict:
    key = jax.random.key(seed)
    ks = jax.random.split(key, 25)
    n_pages = PAST_LEN // PAGE_SIZE
    n_pool = (5 * DEC_BATCH * n_pages + 3) // 4

    def nrm(k, shape, s):
        return s * jax.random.normal(k, shape, jnp.float32)

    page_table = jax.random.permutation(ks[7], n_pool)[:DEC_BATCH * n_pages]
    page_table = page_table.reshape(DEC_BATCH, n_pages).astype(jnp.int32)
    return {
        'x_prompt': nrm(ks[0], (BATCH, SEQ, D_MODEL), 1.0),
        'x_sample': nrm(ks[1], (DEC_BATCH, DEC_SEQ, D_MODEL), 1.0),
        'cache_k': nrm(ks[2], (DEPTH, n_pool, PAGE_SIZE, H_A, D_HA), 1.0),
        'cache_v': nrm(ks[3], (DEPTH, n_pool, PAGE_SIZE, H_A, D_HA), 1.0),
        'state_conv': nrm(ks[4], (DEPTH, DEC_BATCH, CONV_W - 1, W_B), 1.0),
        'state_hgrn': nrm(ks[5], (DEPTH, DEC_BATCH, H_C, DK_C, DV_C), 0.5),
        'state_ffn_conv': nrm(ks[6], (DEPTH, DEC_BATCH, CONV_W - 1, D_FF), 1.0),
        'page_table': page_table,
        'w_in': nrm(ks[8], (DEPTH, D_MODEL, D_IN), D_MODEL ** -0.5),
        'sb_bias': SB_BIAS_INIT + nrm(ks[24], (DEPTH, H_A), 0.3),
        'conv_w': nrm(ks[9], (DEPTH, CONV_W, W_B), CONV_W ** -0.5),
        'hgrn_lb': nrm(ks[10], (DEPTH, W_C), 0.1),
        'hgrn_norm_w': 1.0 + nrm(ks[11], (DEPTH, DV_C), 0.02),
        'w_br_a': nrm(ks[12], (DEPTH, W_A, D_MODEL), BETA * W_A ** -0.5),
        'w_br_b': nrm(ks[13], (DEPTH, W_B, D_MODEL), BETA * W_B ** -0.5),
        'w_br_c': nrm(ks[14], (DEPTH, W_CV, D_MODEL), BETA * W_CV ** -0.5),
        'w_o': nrm(ks[15], (DEPTH, D_MODEL, D_MODEL), BETA * D_MODEL ** -0.5),
        'ln1_g': 1.0 + nrm(ks[16], (DEPTH, D_MODEL), 0.02),
        'ln1_b': nrm(ks[17], (DEPTH, D_MODEL), 0.02),
        'w_up': nrm(ks[18], (DEPTH, D_MODEL, 2 * D_FF), D_MODEL ** -0.5),
        'ffn_conv_w': nrm(ks[19], (DEPTH, CONV_W, D_FF), CONV_W ** -0.5),
        'ffn_conv_b': nrm(ks[20], (DEPTH, D_FF), 0.02),
        'w_down': nrm(ks[21], (DEPTH, D_FF, D_MODEL), BETA * D_FF ** -0.5),
        'ln2_g': 1.0 + nrm(ks[22], (DEPTH, D_MODEL), 0.02),
        'ln2_b': nrm(ks[23], (DEPTH, D_MODEL), 0.02),
    }


def reference(x_prompt, x_sample, cache_k, cache_v, state_conv, state_hgrn, state_ffn_conv,
              page_table, w_in, sb_bias, conv_w, hgrn_lb, hgrn_norm_w, w_br_a, w_br_b, w_br_c, w_o,
              ln1_g, ln1_b, w_up, ffn_conv_w, ffn_conv_b, w_down, ln2_g, ln2_b):
    lbs = jnp.cumsum(jax.nn.softmax(hgrn_lb.astype(jnp.float32), axis=0), axis=0)
    lower = lbs - lbs[0]

    Bp = x_prompt.shape[0]
    Bs, Ts = x_sample.shape[:2]
    past_len = page_table.shape[1] * cache_k.shape[2]
    q_pos = past_len + jnp.arange(Ts)
    k_pos = jnp.arange(past_len + Ts)
    dt = x_prompt.dtype
    zero_conv = jnp.zeros((Bp, CONV_W - 1, W_B), dt)
    zero_hgrn = jnp.zeros((Bp, H_C, DK_C, DV_C), dt)
    zero_ffn = jnp.zeros((Bp, CONV_W - 1, D_FF), dt)

    hp, hs = x_prompt, x_sample
    kp_l, vp_l, cp_l, sp_l, fp_l = [], [], [], [], []
    ks_l, vs_l, cs_l, ss_l, fs_l = [], [], [], [], []
    for l in range(DEPTH):
        weights = (w_in[l], sb_bias[l], conv_w[l], hgrn_norm_w[l], w_br_a[l], w_br_b[l], w_br_c[l],
                   w_o[l], ln1_g[l], ln1_b[l], w_up[l], ffn_conv_w[l], ffn_conv_b[l], w_down[l],
                   ln2_g[l], ln2_b[l])
        hp, kp, vp, cp, sp, fp = trunk_layer(hp, sb_prompt, zero_conv, zero_hgrn, zero_ffn,
                                             lower[l], *weights)
        past_k = cache_k[l][page_table].reshape(Bs, past_len, H_A, D_HA)
        past_v = cache_v[l][page_table].reshape(Bs, past_len, H_A, D_HA)

        def attend_sample(q, k, v, bias, past_k=past_k, past_v=past_v):
            kk = jnp.concatenate([past_k, k.astype(past_k.dtype)], axis=1)
            vv = jnp.concatenate([past_v, v.astype(past_v.dtype)], axis=1)
            return sb_block(q, q_pos, kk, vv, k_pos, bias)

        hs, ks_, vs_, cs, ss, fs = trunk_layer(hs, attend_sample, state_conv[l], state_hgrn[l],
                                               state_ffn_conv[l], lower[l], *weights)
        kp_l.append(kp); vp_l.append(vp); cp_l.append(cp); sp_l.append(sp); fp_l.append(fp)
        ks_l.append(ks_); vs_l.append(vs_); cs_l.append(cs); ss_l.append(ss); fs_l.append(fs)

    return (hp, hs,
            jnp.stack(kp_l), jnp.stack(vp_l), jnp.stack(cp_l), jnp.stack(sp_l), jnp.stack(fp_l),
            jnp.stack(ks_l), jnp.stack(vs_l), jnp.stack(cs_l), jnp.stack(ss_l), jnp.stack(fs_l))
```

```python
import functools
import math

import numpy as np
import jax
import jax.numpy as jnp
from jax import lax
from jax.experimental import pallas as pl
from jax.experimental.pallas import tpu as pltpu

F32 = jnp.float32
BF16 = jnp.bfloat16

LANES = 128
SUBLANES = 8
CONV_TAPS = 3
LN_EPS = 1e-5
RMS_EPS = 1e-6
HG_CHUNK = 128
MERGE_TM = 256
FFN_TM = 1024
VMEM_LIMIT = 56 * 1024 * 1024


def _params(*sem):
    return pltpu.CompilerParams(dimension_semantics=sem, vmem_limit_bytes=VMEM_LIMIT)


def _softplus(z):
    return jnp.maximum(z, 0.0) + jnp.log(1.0 + jnp.exp(-jnp.abs(z)))


def _split2(x):
    hi = x.astype(BF16)
    lo = (x - hi.astype(F32)).astype(BF16)
    return hi, lo


def _dot(a, b):
    return jnp.dot(a, b, preferred_element_type=F32)


def _dot_nt(a, b):
    return lax.dot_general(a, b, (((1,), (1,)), ((), ())), preferred_element_type=F32)


def _mm_kernel(x_ref, w_ref, o_ref):
    o_ref[...] = _dot(x_ref[...], w_ref[...])


def _proj_in(xb, wb):
    n, k = xb.shape
    m = wb.shape[1]
    tm = min(n, 1024)
    tn = 1024
    return pl.pallas_call(
        _mm_kernel,
        grid=(n // tm, m // tn),
        in_specs=[pl.BlockSpec((tm, k), lambda i, j: (i, 0)),
                  pl.BlockSpec((k, tn), lambda i, j: (0, j))],
        out_specs=pl.BlockSpec((tm, tn), lambda i, j: (i, j)),
        out_shape=jax.ShapeDtypeStruct((n, m), F32),
        compiler_params=_params("parallel", "arbitrary"),
        name="proj_in",
    )(xb, wb)


def _sb_block(qh, kb, vb, tri, bias, carry, acc, mask):
    z = _dot_nt(qh, kb) + bias
    sp = _softplus(z)
    if mask is not None:
        sp = jnp.where(mask, sp, 0.0)
    hi, lo = _split2(sp)
    c = _dot(hi, tri) + _dot(lo, tri)
    w = jnp.exp(z - c - carry)
    if mask is not None:
        w = jnp.where(mask, w, 0.0)
    acc = acc + _dot(w.astype(BF16), vb)
    carry = carry + c[:, 0:1]
    return carry, acc


def _sb_prompt_kernel(bias_ref, q_ref, k_ref, v_ref, tri_ref, o_ref, kb_ref, vb_ref, *, tq, scale):
    hp = pl.program_id(1)
    qi = pl.program_id(2)

    @pl.when(qi == 0)
    def _():
        kb_ref[...] = k_ref[...].astype(BF16)
        vb_ref[...] = v_ref[...].astype(BF16)

    lane = lax.broadcasted_iota(jnp.int32, (1, LANES), 1)
    q = q_ref[...] * scale
    row = lax.broadcasted_iota(jnp.int32, (tq, tq), 0)
    col = lax.broadcasted_iota(jnp.int32, (tq, tq), 1)
    causal = col < row
    tri = tri_ref[...]
    half = LANES // 2
    outs = []
    for h in range(2):
        qh = jnp.where((lane >= h * half) & (lane < (h + 1) * half), q, 0.0).astype(BF16)
        bias = bias_ref[2 * hp + h]

        def block(j, carry, acc, mask, qh=qh, bias=bias):
            start = pl.multiple_of(j * tq, tq)
            kb = kb_ref[pl.ds(start, tq), :]
            vb = vb_ref[pl.ds(start, tq), :]
            return _sb_block(qh, kb, vb, tri, bias, carry, acc, mask)

        carry = jnp.zeros((tq, 1), F32)
        acc = jnp.zeros((tq, LANES), F32)
        carry, acc = block(qi, carry, acc, causal)

        def body(i, ca, block=block):
            return block(qi - 1 - i, ca[0], ca[1], None)

        carry, acc = lax.fori_loop(0, qi, body, (carry, acc))
        outs.append(acc)
    o_ref[...] = jnp.where(lane < half, outs[0], outs[1])


def _tri_suffix(n):
    j = np.arange(n)[:, None]
    s = np.arange(n)[None, :]
    return jnp.asarray((j >= s).astype(np.float32), dtype=BF16)


def _sb_prompt(y, sb_bias, batch, seq, n_heads, d_head):
    n = y.shape[0]
    w_a = n_heads * d_head
    tq = 256
    nq = seq // tq
    npair = w_a // LANES
    kern = functools.partial(_sb_prompt_kernel, tq=tq, scale=d_head ** -0.5)
    return pl.pallas_call(
        kern,
        grid_spec=pltpu.PrefetchScalarGridSpec(
            num_scalar_prefetch=1,
            grid=(batch, npair, nq),
            in_specs=[
                pl.BlockSpec((tq, LANES), lambda b, hp, qi, bias: (b * nq + qi, hp)),
                pl.BlockSpec((seq, LANES), lambda b, hp, qi, bias: (b, npair + hp)),
                pl.BlockSpec((seq, LANES), lambda b, hp, qi, bias: (b, 2 * npair + hp)),
                pl.BlockSpec((tq, tq), lambda b, hp, qi, bias: (0, 0)),
            ],
            out_specs=pl.BlockSpec((tq, LANES), lambda b, hp, qi, bias: (b * nq + qi, hp)),
            scratch_shapes=[pltpu.VMEM((seq, LANES), BF16), pltpu.VMEM((seq, LANES), BF16)],
        ),
        out_shape=jax.ShapeDtypeStruct((n, w_a), F32),
        compiler_params=_params("parallel", "parallel", "arbitrary"),
        name="sb_prompt",
    )(sb_bias, y, y, y, _tri_suffix(tq))


def _sb_sample_kernel(pt_ref, q_ref, kn_ref, vn_ref, bias_ref, tri_ref, *rest,
                      pages_per_step, n_heads, d_head, t_new, scale):
    kpages = rest[:pages_per_step]
    vpages = rest[pages_per_step:2 * pages_per_step]
    o_ref, qbd_ref, carry_ref, acc_ref = rest[2 * pages_per_step:]
    p = pl.program_id(1)
    rows = n_heads * t_new
    w_a = n_heads * d_head
    page = tri_ref.shape[0]
    head_mask = (lax.broadcasted_iota(jnp.int32, (rows, w_a), 0) // t_new
                 == lax.broadcasted_iota(jnp.int32, (rows, w_a), 1) // d_head)
    bias = bias_ref[...]
    tri = tri_ref[...]

    @pl.when(p == 0)
    def _():
        q = q_ref[...] * scale
        qbd = jnp.where(head_mask, jnp.concatenate([q] * n_heads, axis=0), 0.0).astype(BF16)
        qbd_ref[...] = qbd
        pad = jnp.zeros((page - t_new, w_a), F32)
        kn = jnp.concatenate([kn_ref[...], pad], axis=0).astype(BF16)
        vn = jnp.concatenate([vn_ref[...], pad], axis=0).astype(BF16)
        mask = (lax.broadcasted_iota(jnp.int32, (rows, page), 1)
                < lax.broadcasted_iota(jnp.int32, (rows, page), 0) % t_new)
        carry, acc = _sb_block(qbd, kn, vn, tri, bias, jnp.zeros((rows, 1), F32),
                               jnp.zeros((rows, w_a), F32), mask)
        carry_ref[...] = carry
        acc_ref[...] = acc

    qbd = qbd_ref[...]
    carry = carry_ref[...]
    acc = acc_ref[...]
    for r in range(pages_per_step):
        kb = kpages[r][...].astype(BF16)
        vb = vpages[r][...].astype(BF16)
        carry, acc = _sb_block(qbd, kb, vb, tri, bias, carry, acc, None)
    carry_ref[...] = carry
    acc_ref[...] = acc

    @pl.when(p == pl.num_programs(1) - 1)
    def _():
        a = jnp.where(head_mask, acc, 0.0)
        out = a[0:t_new]
        for h in range(1, n_heads):
            out = out + a[h * t_new:(h + 1) * t_new]
        o_ref[...] = out


def _sb_sample(ys, cache_k, cache_v, page_table, sb_bias, layer, n_heads, d_head, t_new):
    n = ys.shape[0]
    n_seq, n_pages = page_table.shape
    page = cache_k.shape[2]
    w_a = n_heads * d_head
    rows = n_heads * t_new
    pps = 4 if n_pages % 4 == 0 else 1
    steps = n_pages // pps
    bias_col = jnp.repeat(sb_bias.astype(F32), t_new).reshape(rows, 1)

    def page_spec(r):
        return pl.BlockSpec(
            (None, None, page, w_a),
            lambda b, p, pt, r=r: (layer, pt[b, n_pages - 1 - (p * pps + r)], 0, 0))

    kern = functools.partial(_sb_sample_kernel, pages_per_step=pps, n_heads=n_heads,
                             d_head=d_head, t_new=t_new, scale=d_head ** -0.5)
    return pl.pallas_call(
        kern,
        grid_spec=pltpu.PrefetchScalarGridSpec(
            num_scalar_prefetch=1,
            grid=(n_seq, steps),
            in_specs=[
                pl.BlockSpec((t_new, w_a), lambda b, p, pt: (b, 0)),
                pl.BlockSpec((t_new, w_a), lambda b, p, pt: (b, 1)),
                pl.BlockSpec((t_new, w_a), lambda b, p, pt: (b, 2)),
                pl.BlockSpec((rows, 1), lambda b, p, pt: (0, 0)),
                pl.BlockSpec((page, page), lambda b, p, pt: (0, 0)),
            ] + [page_spec(r) for r in range(pps)] + [page_spec(r) for r in range(pps)],
            out_specs=pl.BlockSpec((t_new, w_a), lambda b, p, pt: (b, 0)),
            scratch_shapes=[pltpu.VMEM((rows, w_a), BF16), pltpu.VMEM((rows, 1), F32),
                            pltpu.VMEM((rows, w_a), F32)],
        ),
        out_shape=jax.ShapeDtypeStruct((n, w_a), F32),
        compiler_params=_params("parallel", "arbitrary"),
        name="sb_sample",
    )(page_table, ys, ys, ys, bias_col, _tri_suffix(page),
      *([cache_k] * pps), *([cache_v] * pps))


def _hgrn_masks(c, seg):
    t = np.arange(c)[:, None]
    j = np.arange(c)[None, :]
    same = (t // seg) == (j // seg)
    halves = []
    h = seg // 2
    while h >= 1:
        halves.append(h)
        h //= 2
    blocks = [same & (j <= t), same & (j > t)]
    ublocks, lblocks, pairs = [], [], []
    for h in halves:
        upper_t = ((t // h) % 2) == 1
        m_t = (t // h) * h
        ublocks.append(upper_t & (j >= m_t) & (j <= t))
        lower_t = ~upper_t
        m_next = (t // h + 1) * h
        lblocks.append(lower_t & (j > t) & (j < m_next))
        s = j
        pairs.append(upper_t & ((((s // h) % 2) == 0)) & ((t // (2 * h)) == (s // (2 * h))))
    pairs.append(t == j)
    m_all = np.concatenate(blocks + ublocks + lblocks, axis=0).astype(np.float32)
    m_cat = np.concatenate([m_all, m_all], axis=1)
    return (jnp.asarray(m_cat, dtype=BF16), jnp.asarray(np.stack(pairs).astype(np.float32)),
            len(halves))


def _hgrn_chunk(q, zf, v, og, lb, norm_w, mcat, pairs, n_levels, state_fn):
    c = q.shape[0]
    log_sig = jnp.minimum(zf, 0.0) - jnp.log(1.0 + jnp.exp(-jnp.abs(zf)))
    a1 = jnp.log(lb)
    a2 = jnp.log(1.0 - lb) + log_sig
    g = jnp.maximum(a1, a2) + jnp.log(1.0 + jnp.exp(-jnp.abs(a1 - a2)))
    k = (1.0 - lb) * jax.nn.sigmoid(-zf)

    ghi, glo = _split2(g)
    e = _dot(mcat, jnp.concatenate([ghi, glo], axis=0))

    def blk(i):
        return e[i * c:(i + 1) * c]

    qb = q.astype(BF16)
    kb = k.astype(BF16)
    vb = v.astype(BF16)
    scores = _dot_nt(qb, kb) * pairs[n_levels]
    for i in range(n_levels):
        qe = (q * jnp.exp(blk(2 + i))).astype(BF16)
        ke = (k * jnp.exp(blk(2 + n_levels + i))).astype(BF16)
        scores = scores + _dot_nt(qe, ke) * pairs[i]
    eb = jnp.exp(blk(0))
    kd = k * jnp.exp(blk(1))
    o = _dot(scores.astype(BF16), vb)
    o = o + state_fn((q * eb).astype(BF16), eb.T, kd.T, vb)
    o = o * lax.rsqrt(jnp.mean(o * o, axis=-1, keepdims=True) + RMS_EPS) * norm_w
    return o * (og * jax.nn.sigmoid(og))


def _hgrn_prompt_kernel(q_ref, f_ref, v_ref, og_ref, lb_ref, nw_ref, mcat_ref, pairs_ref,
                        y_ref, s_out_ref, s_ref, *, n_levels, chunk):
    ti = pl.program_id(2)

    @pl.when(ti == 0)
    def _():
        s_ref[...] = jnp.zeros_like(s_ref)

    lb = lb_ref[...]
    nw = nw_ref[...]
    mcat = mcat_ref[...]
    pairs = pairs_ref[...]
    n_chunks = q_ref.shape[0] // chunk

    def state_fn(qe, eb_t, kd_t, vb):
        s0 = s_ref[...]
        o = _dot(qe, s0.astype(BF16))
        s_ref[...] = s0 * eb_t[:, chunk - 1:chunk] + _dot(kd_t.astype(BF16), vb)
        return o

    for r in range(n_chunks):
        sl = pl.ds(r * chunk, chunk)
        y_ref[sl, :] = _hgrn_chunk(q_ref[sl, :], f_ref[sl, :], v_ref[sl, :], og_ref[sl, :],
                                   lb, nw, mcat, pairs, n_levels, state_fn)

    @pl.when(ti == pl.num_programs(2) - 1)
    def _():
        s_out_ref[...] = s_ref[...]


def _hgrn_sample_kernel(q_ref, f_ref, v_ref, og_ref, lb_ref, nw_ref, mcat_ref, pairs_ref,
                        s0_ref, y_ref, s_out_ref, *, n_levels, seg):
    chunk = q_ref.shape[0]
    n_seq = chunk // seg
    lane_seq = lax.broadcasted_iota(jnp.int32, (1, chunk), 1) // seg
    row_seq = lax.broadcasted_iota(jnp.int32, (chunk, 1), 0) // seg

    def state_fn(qe, eb_t, kd_t, vb):
        o = jnp.zeros((chunk, vb.shape[1]), F32)
        for i in range(n_seq):
            s0 = s0_ref[i]
            o = o + _dot(jnp.where(row_seq == i, qe, jnp.zeros_like(qe)), s0.astype(BF16))
            last = (i + 1) * seg - 1
            kd_i = jnp.where(lane_seq == i, kd_t, 0.0).astype(BF16)
            s_out_ref[i] = s0 * eb_t[:, last:last + 1] + _dot(kd_i, vb)
        return o

    y_ref[...] = _hgrn_chunk(q_ref[...], f_ref[...], v_ref[...], og_ref[...], lb_ref[...],
                             nw_ref[...], mcat_ref[...], pairs_ref[...], n_levels, state_fn)


def _hgrn_col_blocks(sizes, n_heads_c):
    off = sum(sizes[:6]) // LANES
    return off, off + n_heads_c, off + 2 * n_heads_c, off + 3 * n_heads_c


def _hgrn_prompt(y, lower, norm_w, batch, seq, n_heads_c, sizes):
    n = y.shape[0]
    chunk = HG_CHUNK
    tt = 512
    nt = seq // tt
    mcat, pairs, n_levels = _hgrn_masks(chunk, chunk)
    cq, cf, ci, cg = _hgrn_col_blocks(sizes, n_heads_c)

    def col(cb):
        return pl.BlockSpec((tt, LANES), lambda b, h, t, cb=cb: (b * nt + t, cb + h))

    const2 = lambda shape: pl.BlockSpec(shape, lambda b, h, t: (0, 0))
    kern = functools.partial(_hgrn_prompt_kernel, n_levels=n_levels, chunk=chunk)
    return pl.pallas_call(
        kern,
        grid=(batch, n_heads_c, nt),
        in_specs=[col(cq), col(cf), col(ci), col(cg),
                  pl.BlockSpec((1, LANES), lambda b, h, t: (0, h)),
                  const2((1, LANES)), const2(mcat.shape),
                  pl.BlockSpec(pairs.shape, lambda b, h, t: (0, 0, 0))],
        out_specs=[pl.BlockSpec((tt, LANES), lambda b, h, t: (b * nt + t, h)),
                   pl.BlockSpec((None, None, LANES, LANES), lambda b, h, t: (b, h, 0, 0))],
        out_shape=[jax.ShapeDtypeStruct((n, n_heads_c * LANES), F32),
                   jax.ShapeDtypeStruct((batch, n_heads_c, LANES, LANES), F32)],
        scratch_shapes=[pltpu.VMEM((LANES, LANES), F32)],
        compiler_params=_params("parallel", "parallel", "arbitrary"),
        name="hgrn_prompt",
    )(y, y, y, y, lower, norm_w, mcat, pairs)


def _hgrn_sample(ys, lower, norm_w, state, layer, t_new, n_heads_c, sizes):
    n = ys.shape[0]
    chunk = HG_CHUNK
    n_seq = state.shape[1]
    g = chunk // t_new
    mcat, pairs, n_levels = _hgrn_masks(chunk, t_new)
    cq, cf, ci, cg = _hgrn_col_blocks(sizes, n_heads_c)

    def col(cb):
        return pl.BlockSpec((chunk, LANES), lambda i, h, cb=cb: (i, cb + h))

    const2 = lambda shape: pl.BlockSpec(shape, lambda i, h: (0, 0))
    kern = functools.partial(_hgrn_sample_kernel, n_levels=n_levels, seg=t_new)
    return pl.pallas_call(
        kern,
        grid=(n // chunk, n_heads_c),
        in_specs=[col(cq), col(cf), col(ci), col(cg),
                  pl.BlockSpec((1, LANES), lambda i, h: (0, h)),
                  const2((1, LANES)), const2(mcat.shape),
                  pl.BlockSpec(pairs.shape, lambda i, h: (0, 0, 0)),
                  pl.BlockSpec((None, g, None, LANES, LANES), lambda i, h: (layer, i, h, 0, 0))],
        out_specs=[pl.BlockSpec((chunk, LANES), lambda i, h: (i, h)),
                   pl.BlockSpec((g, None, LANES, LANES), lambda i, h: (i, h, 0, 0))],
        out_shape=[jax.ShapeDtypeStruct((n, n_heads_c * LANES), F32),
                   jax.ShapeDtypeStruct((n_seq, n_heads_c, LANES, LANES), F32)],
        compiler_params=_params("parallel", "parallel"),
        name="hgrn_sample",
    )(ys, ys, ys, ys, lower, norm_w, mcat, pairs, state)


def _conv_prompt(u, w, carry_ref):
    tm = u.shape[0]
    r = lax.broadcasted_iota(jnp.int32, (tm, 1), 0)
    c6 = carry_ref[SUBLANES - 2:SUBLANES - 1, :]
    c7 = carry_ref[SUBLANES - 1:SUBLANES, :]
    u1 = jnp.where(r == 0, c7, pltpu.roll(u, 1, 0))
    u2 = jnp.where(r == 0, c6, jnp.where(r == 1, c7, pltpu.roll(u, 2, 0)))
    carry_ref[...] = u[tm - SUBLANES:, :]
    return w[2:3, :] * u + w[1:2, :] * u1 + w[0:1, :] * u2


def _conv_sample(u, w, hist, t_new):
    tm = u.shape[0]
    pos = lax.broadcasted_iota(jnp.int32, (tm, 1), 0) % t_new
    u1 = jnp.where(pos == 0, pltpu.roll(hist, tm - 1, 0), pltpu.roll(u, 1, 0))
    u2 = jnp.where(pos < 2, hist, pltpu.roll(u, 2, 0))
    return w[2:3, :] * u + w[1:2, :] * u1 + w[0:1, :] * u2


def _layer_norm(h, g, b):
    hc = h - jnp.mean(h, axis=-1, keepdims=True)
    var = jnp.mean(hc * hc, axis=-1, keepdims=True)
    return hc * lax.rsqrt(var + LN_EPS) * g + b


def _merge_kernel(x_ref, gb_ref, gc_ref, hb_ref, ma_ref, mb_ref, mc_ref, ya_ref, yc_ref,
                  cw_ref, wa_ref, wb_ref, wc_ref, wo_ref, g_ref, b_ref, *rest,
                  sample, t_new, tiles_per_seq, alpha):
    if sample:
        hist_ref, x1_ref, x1b_ref, u_ref = rest
    else:
        x1_ref, x1b_ref, tail_ref, carry_ref = rest

        @pl.when(pl.program_id(0) % tiles_per_seq == 0)
        def _():
            carry_ref[...] = jnp.zeros_like(carry_ref)

    u = gc_ref[...] * hb_ref[...]
    cw = cw_ref[...]
    if sample:
        conv = _conv_sample(u, cw, hist_ref[...], t_new)
        u_ref[...] = u
    else:
        conv = _conv_prompt(u, cw, carry_ref)
        tail_ref[...] = u[u.shape[0] - SUBLANES:, :]
    yb = gb_ref[...] * conv
    mix = (jax.nn.sigmoid(ma_ref[...]) * _dot(ya_ref[...].astype(BF16), wa_ref[...])
           + jax.nn.sigmoid(mb_ref[...]) * _dot(yb.astype(BF16), wb_ref[...])
           + jax.nn.sigmoid(mc_ref[...]) * _dot(yc_ref[...].astype(BF16), wc_ref[...]))
    h = alpha * x_ref[...] + _dot(mix.astype(BF16), wo_ref[...])
    x1 = _layer_norm(h, g_ref[...], b_ref[...])
    x1_ref[...] = x1
    x1b_ref[...] = x1.astype(BF16)


def _merge(x, y, ya, yc, conv_w, wa, wb, wc, wo, ln_g, ln_b, sizes, alpha, seq, hist=None,
           t_new=None):
    n, d = x.shape
    w_b = sizes[3]
    sample = hist is not None
    tm = min(n, MERGE_TM)
    nt = n // tm
    off_b = sum(sizes[:3]) // w_b
    off_m = sum(sizes[:10]) // d
    row = lambda width, cb: pl.BlockSpec((tm, width), lambda i, cb=cb: (i, cb))
    const = lambda a: pl.BlockSpec(a.shape, lambda i: (0, 0))
    in_specs = [row(d, 0), row(w_b, off_b), row(w_b, off_b + 1), row(w_b, off_b + 2),
                row(d, off_m), row(d, off_m + 1), row(d, off_m + 2),
                row(ya.shape[1], 0), row(yc.shape[1], 0),
                const(conv_w), const(wa), const(wb), const(wc), const(wo), const(ln_g), const(ln_b)]
    args = [x, y, y, y, y, y, y, ya, yc, conv_w, wa, wb, wc, wo, ln_g, ln_b]
    out_specs = [row(d, 0), row(d, 0)]
    out_shape = [jax.ShapeDtypeStruct((n, d), F32), jax.ShapeDtypeStruct((n, d), BF16)]
    scratch = []
    if sample:
        in_specs.append(row(w_b, 0))
        args.append(hist)
        out_specs.append(row(w_b, 0))
        out_shape.append(jax.ShapeDtypeStruct((n, w_b), F32))
    else:
        out_specs.append(pl.BlockSpec((SUBLANES, w_b), lambda i: (i, 0)))
        out_shape.append(jax.ShapeDtypeStruct((nt * SUBLANES, w_b), F32))
        scratch.append(pltpu.VMEM((SUBLANES, w_b), F32))
    kern = functools.partial(_merge_kernel, sample=sample, t_new=t_new,
                             tiles_per_seq=max(seq // tm, 1), alpha=alpha)
    return pl.pallas_call(
        kern, grid=(nt,), in_specs=in_specs, out_specs=out_specs, out_shape=out_shape,
        scratch_shapes=scratch, compiler_params=_params("arbitrary"),
        name="merge_sample" if sample else "merge_prompt",
    )(*args)


def _ffn_kernel(xb_ref, x_ref, wa_ref, wg_ref, cw_ref, cb_ref, wd_ref, g_ref, b_ref, *rest,
                sample, t_new, tiles_per_seq, alpha):
    if sample:
        hist_ref, o_ref, ob_ref, a_ref, acc_ref = rest
    else:
        o_ref, ob_ref, tail_ref, acc_ref, carry_ref = rest
    i = pl.program_id(0)
    f = pl.program_id(1)

    @pl.when(f == 0)
    def _():
        acc_ref[...] = jnp.zeros_like(acc_ref)

    xb = xb_ref[...]
    a = _dot(xb, wa_ref[...])
    gate = _dot(xb, wg_ref[...])
    cw = cw_ref[...]
    if sample:
        conv = _conv_sample(a, cw, hist_ref[...], t_new)
        a_ref[...] = a
    else:
        @pl.when(i % tiles_per_seq == 0)
        def _():
            carry_ref[f] = jnp.zeros(carry_ref.shape[1:], F32)

        conv = _conv_prompt(a, cw, carry_ref.at[f])
        tail_ref[...] = a[a.shape[0] - SUBLANES:, :]
    h = jax.nn.gelu(conv + cb_ref[...]) * gate
    acc_ref[...] += _dot(h.astype(BF16), wd_ref[...])

    @pl.when(f == pl.num_programs(1) - 1)
    def _():
        x2 = _layer_norm(alpha * x_ref[...] + acc_ref[...], g_ref[...], b_ref[...])
        o_ref[...] = x2
        ob_ref[...] = x2.astype(BF16)


def _ffn(xb, x, w_up, conv_w, conv_b, w_down, ln_g, ln_b, alpha, seq, hist=None, t_new=None):
    n, d = x.shape
    d_ff = w_down.shape[0]
    sample = hist is not None
    tm = min(n, FFN_TM)
    tf = 256
    nt = n // tm
    nf = d_ff // tf
    in_specs = [pl.BlockSpec((tm, d), lambda i, f: (i, 0)),
                pl.BlockSpec((tm, d), lambda i, f: (i, 0)),
                pl.BlockSpec((d, tf), lambda i, f: (0, f)),
                pl.BlockSpec((d, tf), lambda i, f: (0, nf + f)),
                pl.BlockSpec((CONV_TAPS, tf), lambda i, f: (0, f)),
                pl.BlockSpec((1, tf), lambda i, f: (0, f)),
                pl.BlockSpec((tf, d), lambda i, f: (f, 0)),
                pl.BlockSpec((1, d), lambda i, f: (0, 0)),
                pl.BlockSpec((1, d), lambda i, f: (0, 0))]
    args = [xb, x, w_up, w_up, conv_w, conv_b, w_down, ln_g, ln_b]
    out_specs = [pl.BlockSpec((tm, d), lambda i, f: (i, 0)),
                 pl.BlockSpec((tm, d), lambda i, f: (i, 0))]
    out_shape = [jax.ShapeDtypeStruct((n, d), F32), jax.ShapeDtypeStruct((n, d), BF16)]
    scratch = [pltpu.VMEM((tm, d), F32)]
    if sample:
        in_specs.append(pl.BlockSpec((tm, tf), lambda i, f: (i, f)))
        args.append(hist)
        out_specs.append(pl.BlockSpec((tm, tf), lambda i, f: (i, f)))
        out_shape.append(jax.ShapeDtypeStruct((n, d_ff), F32))
    else:
        out_specs.append(pl.BlockSpec((SUBLANES, tf), lambda i, f: (i, f)))
        out_shape.append(jax.ShapeDtypeStruct((nt * SUBLANES, d_ff), F32))
        scratch.append(pltpu.VMEM((nf, SUBLANES, tf), F32))
    kern = functools.partial(_ffn_kernel, sample=sample, t_new=t_new,
                             tiles_per_seq=max(seq // tm, 1), alpha=alpha)
    return pl.pallas_call(
        kern, grid=(nt, nf), in_specs=in_specs, out_specs=out_specs, out_shape=out_shape,
        scratch_shapes=scratch, compiler_params=_params("arbitrary", "arbitrary"),
        name="ffn_sample" if sample else "ffn_prompt",
    )(*args)


def _pad_hist(state, t_new):
    n_seq, nb, c = state.shape
    return jnp.pad(state, ((0, 0), (0, t_new - nb), (0, 0))).reshape(n_seq * t_new, c)


def _tails(tail, batch, tiles_per_seq):
    c = tail.shape[1]
    t = tail.reshape(batch, tiles_per_seq, SUBLANES, c)
    return t[:, tiles_per_seq - 1, SUBLANES - 2:, :]


def kernel(x_prompt, x_sample, cache_k, cache_v, state_conv, state_hgrn, state_ffn_conv, page_table,
           w_in, sb_bias, conv_w, hgrn_lb, hgrn_norm_w, w_br_a, w_br_b, w_br_c, w_o, ln1_g, ln1_b,
           w_up, ffn_conv_w, ffn_conv_b, w_down, ln2_g, ln2_b):
    depth, d_model, d_in = w_in.shape
    batch, seq, _ = x_prompt.shape
    n_seq, t_new, _ = x_sample.shape
    n_heads, d_head = cache_k.shape[3], cache_k.shape[4]
    w_a = n_heads * d_head
    w_b = conv_w.shape[2]
    n_heads_c, dk_c, dv_c = state_hgrn.shape[2:]
    w_c = n_heads_c * dk_c
    w_cv = n_heads_c * dv_c
    d_ff = w_down.shape[1]
    sizes = (w_a, w_a, w_a, w_b, w_b, w_b, w_c, w_c, w_cv, w_cv, d_model, d_model, d_model)
    assert sum(sizes) == d_in and dk_c == LANES and dv_c == LANES and 2 * d_head == LANES
    alpha = (2 * depth) ** 0.25
    n_pool, page = cache_k.shape[1], cache_k.shape[2]

    lbs = jnp.cumsum(jax.nn.softmax(hgrn_lb.astype(F32), axis=0), axis=0)
    lower = lbs - lbs[0]

    ck = cache_k.reshape(depth, n_pool, page, w_a)
    cv = cache_v.reshape(depth, n_pool, page, w_a)
    bf = lambda a: a.astype(BF16)
    w_in_b, wa_b, wb_b, wc_b, wo_b, wu_b, wd_b = map(bf, (w_in, w_br_a, w_br_b, w_br_c, w_o, w_up, w_down))

    hp = x_prompt.reshape(batch * seq, d_model)
    hs = x_sample.reshape(n_seq * t_new, d_model)
    hpb, hsb = bf(hp), bf(hs)
    outs = [[] for _ in range(10)]
    for l in range(depth):
        row = lambda a: a[l].reshape(1, -1)
        yp = _proj_in(hpb, w_in_b[l])
        ys = _proj_in(hsb, w_in_b[l])
        ya_p = _sb_prompt(yp, sb_bias[l], batch, seq, n_heads, d_head)
        ya_s = _sb_sample(ys, ck, cv, page_table, sb_bias[l], l, n_heads, d_head, t_new)
        yc_p, s_p = _hgrn_prompt(yp, row(lower), row(hgrn_norm_w), batch, seq, n_heads_c, sizes)
        yc_s, s_s = _hgrn_sample(ys, row(lower), row(hgrn_norm_w), state_hgrn, l, t_new, n_heads_c, sizes)
        margs = (conv_w[l], wa_b[l], wb_b[l], wc_b[l], wo_b[l], row(ln1_g), row(ln1_b), sizes, alpha)
        x1p, x1pb, tail_b = _merge(hp, yp, ya_p, yc_p, *margs, seq)
        x1s, x1sb, u_s = _merge(hs, ys, ya_s, yc_s, *margs, t_new,
                                hist=_pad_hist(state_conv[l], t_new), t_new=t_new)
        fargs = (wu_b[l], ffn_conv_w[l], row(ffn_conv_b), wd_b[l], row(ln2_g), row(ln2_b), alpha)
        hp, hpb, tail_f = _ffn(x1pb, x1p, *fargs, seq)
        hs, hsb, a_s = _ffn(x1sb, x1s, *fargs, t_new,
                            hist=_pad_hist(state_ffn_conv[l], t_new), t_new=t_new)

        new = (yp[:, w_a:2 * w_a].reshape(batch, seq, n_heads, d_head),
               yp[:, 2 * w_a:3 * w_a].reshape(batch, seq, n_heads, d_head),
               _tails(tail_b, batch, seq // min(batch * seq, MERGE_TM)),
               s_p,
               _tails(tail_f, batch, seq // min(batch * seq, FFN_TM)),
               ys[:, w_a:2 * w_a].reshape(n_seq, t_new, n_heads, d_head),
               ys[:, 2 * w_a:3 * w_a].reshape(n_seq, t_new, n_heads, d_head),
               u_s.reshape(n_seq, t_new, w_b)[:, t_new - 2:],
               s_s,
               a_s.reshape(n_seq, t_new, d_ff)[:, t_new - 2:])
        for o, v in zip(outs, new):
            o.append(v)

    return (hp.reshape(batch, seq, d_model), hs.reshape(n_seq, t_new, d_model),
            *[jnp.stack(o) for o in outs])
```

```python
import functools
import math

import numpy as np
import jax
import jax.numpy as jnp
from jax import lax
from jax.experimental import pallas as pl
from jax.experimental.pallas import tpu as pltpu

F32 = jnp.float32
BF16 = jnp.bfloat16

LANES = 128
SUBLANES = 8
CONV_TAPS = 3
LN_EPS = 1e-5
RMS_EPS = 1e-6
LOG2E = math.log2(math.e)
HG_CHUNK = 128
MERGE_TM = 256
FFN_TM = 1024
VMEM_LIMIT = 56 * 1024 * 1024


def _params(*sem):
    return pltpu.CompilerParams(dimension_semantics=sem, vmem_limit_bytes=VMEM_LIMIT)


def _split2(x):
    hi = x.astype(BF16)
    lo = (x - hi.astype(F32)).astype(BF16)
    return hi, lo


def _dot(a, b):
    return jnp.dot(a, b, preferred_element_type=F32)


def _dot_nt(a, b):
    return lax.dot_general(a, b, (((1,), (1,)), ((), ())), preferred_element_type=F32)


def _mm_kernel(x_ref, w_ref, o_ref):
    o_ref[...] = _dot(x_ref[...], w_ref[...])


def _proj_in(xb, wb):
    n, k = xb.shape
    m = wb.shape[1]
    tm = min(n, 1024)
    tn = 1024
    return pl.pallas_call(
        _mm_kernel,
        grid=(n // tm, m // tn),
        in_specs=[pl.BlockSpec((tm, k), lambda i, j: (i, 0)),
                  pl.BlockSpec((k, tn), lambda i, j: (0, j))],
        out_specs=pl.BlockSpec((tm, tn), lambda i, j: (i, j)),
        out_shape=jax.ShapeDtypeStruct((n, m), F32),
        compiler_params=_params("parallel", "arbitrary"),
        name="proj_in",
    )(xb, wb)


def _mm_nt_kernel(w_ref, x_ref, o_ref):
    o_ref[...] = _dot_nt(w_ref[...], x_ref[...])


def _proj_kv_t(wt, xb, batch, seq):
    m, k = wt.shape
    tm = min(seq, 1024)
    nt = seq // tm
    return pl.pallas_call(
        _mm_nt_kernel,
        grid=(batch, nt),
        in_specs=[pl.BlockSpec((m, k), lambda b, t: (0, 0)),
                  pl.BlockSpec((tm, k), lambda b, t: (b * nt + t, 0))],
        out_specs=pl.BlockSpec((None, m, tm), lambda b, t: (b, 0, t)),
        out_shape=jax.ShapeDtypeStruct((batch, m, seq), F32),
        compiler_params=_params("parallel", "arbitrary"),
        name="proj_kv_t",
    )(wt, xb)


def _softplus2(z):
    neg_abs = lax.bitcast_convert_type(
        lax.bitcast_convert_type(z, jnp.uint32) | jnp.uint32(0x80000000), F32)
    return jnp.maximum(z, 0.0) + jnp.log2(1.0 + jnp.exp2(neg_abs))


def _suffix_sums(sp, tri2):
    n = sp.shape[1]
    hi, lo = _split2(sp)
    c = _dot(jnp.concatenate([hi, lo], axis=1), tri2)
    return c[:, :n], c[:, n:]


def _tri_suffix2(n):
    j = np.arange(n)[:, None]
    s = np.arange(n)[None, :]
    ext = np.concatenate([(j >= s).astype(np.float32), np.ones((n, LANES), np.float32)], axis=1)
    return jnp.asarray(np.concatenate([ext, ext], axis=0), dtype=BF16)


def _sb_prompt_kernel(bias_ref, q_ref, kt_ref, vt_ref, tri_ref, o_ref, kb_ref, vb_ref, carry_ref,
                      acc_ref, *, tq, c2):
    hp = pl.program_id(1)
    qi = pl.program_id(2)
    n_blk = kb_ref.shape[0]
    sub = tri_ref.shape[0] // 2

    @pl.when(qi == 0)
    def _():
        for j in range(n_blk):
            kb_ref[j] = kt_ref[:, j * tq:(j + 1) * tq].astype(BF16)
            vb_ref[j] = vt_ref[:, j * tq:(j + 1) * tq].astype(BF16)

    lane = lax.broadcasted_iota(jnp.int32, (1, LANES), 1)
    half = LANES // 2
    q = q_ref[...] * c2
    qh = [jnp.where((lane >= h * half) & (lane < (h + 1) * half), q, 0.0).astype(BF16)
          for h in range(2)]
    bias2 = [bias_ref[2 * hp + h] * LOG2E for h in range(2)]
    tri2 = tri_ref[...]
    carry_ref[...] = jnp.zeros_like(carry_ref)
    acc_ref[...] = jnp.zeros_like(acc_ref)

    def span(j, mask):
        kb = kb_ref[j]
        vb = vb_ref[j]
        for h in range(2):
            z = _dot(qh[h], kb) + bias2[h]
            sp = _softplus2(z)
            if mask is not None:
                sp = jnp.where(mask, sp, 0.0)
            sums = [_suffix_sums(sp[:, i * sub:(i + 1) * sub], tri2) for i in range(tq // sub)]
            carry = carry_ref[h]
            parts = []
            for i in reversed(range(tq // sub)):
                c, tot = sums[i]
                parts.append(z[:, i * sub:(i + 1) * sub] - c - carry)
                carry = carry + tot
            carry_ref[h] = carry
            w = jnp.exp2(jnp.concatenate(parts[::-1], axis=1))
            if mask is not None:
                w = jnp.where(mask, w, 0.0)
            acc_ref[h] += _dot_nt(w.astype(BF16), vb)

    row = lax.broadcasted_iota(jnp.int32, (tq, tq), 0)
    col = lax.broadcasted_iota(jnp.int32, (tq, tq), 1)
    span(qi, col < row)

    def body(i, _):
        span(qi - 1 - i, None)
        return 0

    lax.fori_loop(0, qi, body, 0)
    o_ref[...] = jnp.where(lane < half, acc_ref[0], acc_ref[1])


def _sb_prompt(q_src, kvt, sb_bias, batch, seq, n_heads, d_head):
    n = q_src.shape[0]
    w_a = n_heads * d_head
    tq = 512
    nq = seq // tq
    npair = w_a // LANES
    tri2 = _tri_suffix2(LANES)
    kern = functools.partial(_sb_prompt_kernel, tq=tq, c2=LOG2E * d_head ** -0.5)
    return pl.pallas_call(
        kern,
        grid_spec=pltpu.PrefetchScalarGridSpec(
            num_scalar_prefetch=1,
            grid=(batch, npair, nq),
            in_specs=[
                pl.BlockSpec((tq, LANES), lambda b, hp, qi, bias: (b * nq + qi, hp)),
                pl.BlockSpec((None, LANES, seq), lambda b, hp, qi, bias: (b, hp, 0)),
                pl.BlockSpec((None, LANES, seq), lambda b, hp, qi, bias: (b, npair + hp, 0)),
                pl.BlockSpec(tri2.shape, lambda b, hp, qi, bias: (0, 0)),
            ],
            out_specs=pl.BlockSpec((tq, LANES), lambda b, hp, qi, bias: (b * nq + qi, hp)),
            scratch_shapes=[pltpu.VMEM((nq, LANES, tq), BF16), pltpu.VMEM((nq, LANES, tq), BF16),
                            pltpu.VMEM((2, tq, LANES), F32), pltpu.VMEM((2, tq, LANES), F32)],
        ),
        out_shape=jax.ShapeDtypeStruct((n, w_a), F32),
        compiler_params=_params("parallel", "parallel", "arbitrary"),
        name="sb_prompt",
    )(sb_bias, q_src, kvt, kvt, tri2)


def _sb_sample_kernel(pt_ref, q_ref, kn_ref, vn_ref, bias_ref, tri_ref, *rest,
                      n_pages, n_heads, d_head, t_new, c2):
    kpages = rest[:n_pages]
    vpages = rest[n_pages:2 * n_pages]
    o_ref = rest[2 * n_pages]
    rows = n_heads * t_new
    w_a = n_heads * d_head
    page = tri_ref.shape[0] // 2
    head_mask = (lax.broadcasted_iota(jnp.int32, (rows, w_a), 0) // t_new
                 == lax.broadcasted_iota(jnp.int32, (rows, w_a), 1) // d_head)
    bias2 = bias_ref[...] * LOG2E
    tri2 = tri_ref[...]

    q = q_ref[...] * c2
    qbd = jnp.where(head_mask, jnp.concatenate([q] * n_heads, axis=0), 0.0).astype(BF16)

    pad = jnp.zeros((page - t_new, w_a), F32)
    kn = jnp.concatenate([kn_ref[...], pad], axis=0).astype(BF16)
    vn = jnp.concatenate([vn_ref[...], pad], axis=0).astype(BF16)
    mask = (lax.broadcasted_iota(jnp.int32, (rows, page), 1)
            < lax.broadcasted_iota(jnp.int32, (rows, page), 0) % t_new)
    zs = [_dot_nt(qbd, kn) + bias2]
    zs += [_dot(qbd, kpages[r][...].astype(BF16)) + bias2 for r in range(n_pages)]
    sps = [jnp.where(mask, _softplus2(zs[0]), 0.0)] + [_softplus2(z) for z in zs[1:]]
    c_all = _dot(jnp.concatenate([jnp.concatenate(_split2(sp), axis=1) for sp in sps], axis=0), tri2)
    carry = jnp.zeros((rows, LANES), F32)
    acc = jnp.zeros((rows, w_a), F32)
    for r in range(n_pages + 1):
        c = c_all[r * rows:(r + 1) * rows]
        w = jnp.exp2(zs[r] - c[:, :page] - carry)
        carry = carry + c[:, page:]
        if r == 0:
            acc = acc + _dot(jnp.where(mask, w, 0.0).astype(BF16), vn)
        else:
            acc = acc + _dot_nt(w.astype(BF16), vpages[r - 1][...].astype(BF16))
    a = jnp.where(head_mask, acc, 0.0)
    out = a[0:t_new]
    for h in range(1, n_heads):
        out = out + a[h * t_new:(h + 1) * t_new]
    o_ref[...] = out


def _sb_sample(ys, ckt, cvt, page_table, sb_bias, layer, n_heads, d_head, t_new):
    n = ys.shape[0]
    n_seq, n_pages = page_table.shape
    page = ckt.shape[3]
    w_a = n_heads * d_head
    rows = n_heads * t_new
    bias_rep = jnp.broadcast_to(jnp.repeat(sb_bias.astype(F32), t_new)[:, None], (rows, LANES))
    tri2 = _tri_suffix2(page)

    def page_spec(r):
        return pl.BlockSpec((None, None, w_a, page),
                            lambda b, pt, r=r: (layer, pt[b, n_pages - 1 - r], 0, 0))

    kern = functools.partial(_sb_sample_kernel, n_pages=n_pages, n_heads=n_heads, d_head=d_head,
                             t_new=t_new, c2=LOG2E * d_head ** -0.5)
    return pl.pallas_call(
        kern,
        grid_spec=pltpu.PrefetchScalarGridSpec(
            num_scalar_prefetch=1,
            grid=(n_seq,),
            in_specs=[
                pl.BlockSpec((t_new, w_a), lambda b, pt: (b, 0)),
                pl.BlockSpec((t_new, w_a), lambda b, pt: (b, 1)),
                pl.BlockSpec((t_new, w_a), lambda b, pt: (b, 2)),
                pl.BlockSpec((rows, LANES), lambda b, pt: (0, 0)),
                pl.BlockSpec(tri2.shape, lambda b, pt: (0, 0)),
            ] + [page_spec(r) for r in range(n_pages)] * 2,
            out_specs=pl.BlockSpec((t_new, w_a), lambda b, pt: (b, 0)),
        ),
        out_shape=jax.ShapeDtypeStruct((n, w_a), F32),
        compiler_params=_params("arbitrary"),
        name="sb_sample",
    )(page_table, ys, ys, ys, bias_rep, tri2, *([ckt] * n_pages), *([cvt] * n_pages))


def _hgrn_masks(c, seg):
    t = np.arange(c)[:, None]
    j = np.arange(c)[None, :]
    same = (t // seg) == (j // seg)
    halves = []
    h = seg // 2
    while h >= 1:
        halves.append(h)
        h //= 2
    blocks = [same & (j <= t), same & (j > t)]
    ublocks, lblocks, pairs = [], [], []
    for h in halves:
        upper_t = ((t // h) % 2) == 1
        m_t = (t // h) * h
        ublocks.append(upper_t & (j >= m_t) & (j <= t))
        lower_t = ~upper_t
        m_next = (t // h + 1) * h
        lblocks.append(lower_t & (j > t) & (j < m_next))
        s = j
        pairs.append(upper_t & ((((s // h) % 2) == 0)) & ((t // (2 * h)) == (s // (2 * h))))
    pairs.append(t == j)
    m_all = np.concatenate(blocks + ublocks + lblocks, axis=0).astype(np.float32)
    m_cat = np.concatenate([m_all, m_all], axis=1)
    return (jnp.asarray(m_cat, dtype=BF16), jnp.asarray(np.stack(pairs).astype(np.float32)),
            len(halves))


def _hgrn_chunk(q, zf, v, og, lb, norm_w, mcat, pairs, n_levels, state_fn):
    c = q.shape[0]
    log_sig = jnp.minimum(zf, 0.0) - jnp.log(1.0 + jnp.exp(-jnp.abs(zf)))
    a1 = jnp.log(lb)
    a2 = jnp.log(1.0 - lb) + log_sig
    g = jnp.maximum(a1, a2) + jnp.log(1.0 + jnp.exp(-jnp.abs(a1 - a2)))
    k = (1.0 - lb) * jax.nn.sigmoid(-zf)

    ghi, glo = _split2(g)
    e = _dot(mcat, jnp.concatenate([ghi, glo], axis=0))

    def blk(i):
        return e[i * c:(i + 1) * c]

    qb = q.astype(BF16)
    kb = k.astype(BF16)
    vb = v.astype(BF16)
    scores = _dot_nt(qb, kb) * pairs[n_levels]
    for i in range(n_levels):
        qe = (q * jnp.exp(blk(2 + i))).astype(BF16)
        ke = (k * jnp.exp(blk(2 + n_levels + i))).astype(BF16)
        scores = scores + _dot_nt(qe, ke) * pairs[i]
    eb = jnp.exp(blk(0))
    kd = k * jnp.exp(blk(1))
    o = _dot(scores.astype(BF16), vb)
    o = o + state_fn((q * eb).astype(BF16), eb.T, kd.T, vb)
    o = o * lax.rsqrt(jnp.mean(o * o, axis=-1, keepdims=True) + RMS_EPS) * norm_w
    return o * (og * jax.nn.sigmoid(og))


def _hgrn_prompt_kernel(q_ref, f_ref, v_ref, og_ref, lb_ref, nw_ref, mcat_ref, pairs_ref,
                        y_ref, s_out_ref, s_ref, *, n_levels, chunk):
    ti = pl.program_id(2)

    @pl.when(ti == 0)
    def _():
        s_ref[...] = jnp.zeros_like(s_ref)

    lb = lb_ref[...]
    nw = nw_ref[...]
    mcat = mcat_ref[...]
    pairs = pairs_ref[...]
    n_chunks = q_ref.shape[0] // chunk

    def state_fn(qe, eb_t, kd_t, vb):
        s0 = s_ref[...]
        o = _dot(qe, s0.astype(BF16))
        s_ref[...] = s0 * eb_t[:, chunk - 1:chunk] + _dot(kd_t.astype(BF16), vb)
        return o

    for r in range(n_chunks):
        sl = pl.ds(r * chunk, chunk)
        y_ref[sl, :] = _hgrn_chunk(q_ref[sl, :], f_ref[sl, :], v_ref[sl, :], og_ref[sl, :],
                                   lb, nw, mcat, pairs, n_levels, state_fn)

    @pl.when(ti == pl.num_programs(2) - 1)
    def _():
        s_out_ref[...] = s_ref[...]


def _hgrn_sample_kernel(q_ref, f_ref, v_ref, og_ref, lb_ref, nw_ref, mcat_ref, pairs_ref,
                        s0_ref, y_ref, s_out_ref, *, n_levels, seg):
    chunk = q_ref.shape[0]
    n_seq = chunk // seg
    lane_seq = lax.broadcasted_iota(jnp.int32, (1, chunk), 1) // seg
    row_seq = lax.broadcasted_iota(jnp.int32, (chunk, 1), 0) // seg

    def state_fn(qe, eb_t, kd_t, vb):
        o = jnp.zeros((chunk, vb.shape[1]), F32)
        for i in range(n_seq):
            s0 = s0_ref[i]
            o = o + _dot(jnp.where(row_seq == i, qe, jnp.zeros_like(qe)), s0.astype(BF16))
            last = (i + 1) * seg - 1
            kd_i = jnp.where(lane_seq == i, kd_t, 0.0).astype(BF16)
            s_out_ref[i] = s0 * eb_t[:, last:last + 1] + _dot(kd_i, vb)
        return o

    y_ref[...] = _hgrn_chunk(q_ref[...], f_ref[...], v_ref[...], og_ref[...], lb_ref[...],
                             nw_ref[...], mcat_ref[...], pairs_ref[...], n_levels, state_fn)


def _col_starts(names, sizes):
    starts, off = {}, 0
    for name in names:
        starts[name] = off
        off += sizes[name]
    return starts


def _hgrn_col_blocks(cols):
    return tuple(cols[k] // LANES for k in ("q_c", "f_c", "i_c", "og_c"))


def _hgrn_prompt(y, lower, norm_w, batch, seq, n_heads_c, cols):
    n = y.shape[0]
    chunk = HG_CHUNK
    tt = 512
    nt = seq // tt
    mcat, pairs, n_levels = _hgrn_masks(chunk, chunk)
    cq, cf, ci, cg = _hgrn_col_blocks(cols)

    def col(cb):
        return pl.BlockSpec((tt, LANES), lambda b, h, t, cb=cb: (b * nt + t, cb + h))

    const2 = lambda shape: pl.BlockSpec(shape, lambda b, h, t: (0, 0))
    kern = functools.partial(_hgrn_prompt_kernel, n_levels=n_levels, chunk=chunk)
    return pl.pallas_call(
        kern,
        grid=(batch, n_heads_c, nt),
        in_specs=[col(cq), col(cf), col(ci), col(cg),
                  pl.BlockSpec((1, LANES), lambda b, h, t: (0, h)),
                  const2((1, LANES)), const2(mcat.shape),
                  pl.BlockSpec(pairs.shape, lambda b, h, t: (0, 0, 0))],
        out_specs=[pl.BlockSpec((tt, LANES), lambda b, h, t: (b * nt + t, h)),
                   pl.BlockSpec((None, None, LANES, LANES), lambda b, h, t: (b, h, 0, 0))],
        out_shape=[jax.ShapeDtypeStruct((n, n_heads_c * LANES), F32),
                   jax.ShapeDtypeStruct((batch, n_heads_c, LANES, LANES), F32)],
        scratch_shapes=[pltpu.VMEM((LANES, LANES), F32)],
        compiler_params=_params("parallel", "parallel", "arbitrary"),
        name="hgrn_prompt",
    )(y, y, y, y, lower, norm_w, mcat, pairs)


def _hgrn_sample(ys, lower, norm_w, state, layer, t_new, n_heads_c, cols):
    n = ys.shape[0]
    chunk = HG_CHUNK
    n_seq = state.shape[1]
    g = chunk // t_new
    mcat, pairs, n_levels = _hgrn_masks(chunk, t_new)
    cq, cf, ci, cg = _hgrn_col_blocks(cols)

    def col(cb):
        return pl.BlockSpec((chunk, LANES), lambda i, h, cb=cb: (i, cb + h))

    const2 = lambda shape: pl.BlockSpec(shape, lambda i, h: (0, 0))
    kern = functools.partial(_hgrn_sample_kernel, n_levels=n_levels, seg=t_new)
    return pl.pallas_call(
        kern,
        grid=(n // chunk, n_heads_c),
        in_specs=[col(cq), col(cf), col(ci), col(cg),
                  pl.BlockSpec((1, LANES), lambda i, h: (0, h)),
                  const2((1, LANES)), const2(mcat.shape),
                  pl.BlockSpec(pairs.shape, lambda i, h: (0, 0, 0)),
                  pl.BlockSpec((None, g, None, LANES, LANES), lambda i, h: (layer, i, h, 0, 0))],
        out_specs=[pl.BlockSpec((chunk, LANES), lambda i, h: (i, h)),
                   pl.BlockSpec((g, None, LANES, LANES), lambda i, h: (i, h, 0, 0))],
        out_shape=[jax.ShapeDtypeStruct((n, n_heads_c * LANES), F32),
                   jax.ShapeDtypeStruct((n_seq, n_heads_c, LANES, LANES), F32)],
        compiler_params=_params("parallel", "parallel"),
        name="hgrn_sample",
    )(ys, ys, ys, ys, lower, norm_w, mcat, pairs, state)


def _conv_prompt(u, w, carry_ref):
    tm = u.shape[0]
    r = lax.broadcasted_iota(jnp.int32, (tm, 1), 0)
    c6 = carry_ref[SUBLANES - 2:SUBLANES - 1, :]
    c7 = carry_ref[SUBLANES - 1:SUBLANES, :]
    u1 = jnp.where(r == 0, c7, pltpu.roll(u, 1, 0))
    u2 = jnp.where(r == 0, c6, jnp.where(r == 1, c7, pltpu.roll(u, 2, 0)))
    carry_ref[...] = u[tm - SUBLANES:, :]
    return w[2:3, :] * u + w[1:2, :] * u1 + w[0:1, :] * u2


def _conv_sample(u, w, hist, t_new):
    tm = u.shape[0]
    pos = lax.broadcasted_iota(jnp.int32, (tm, 1), 0) % t_new
    u1 = jnp.where(pos == 0, pltpu.roll(hist, tm - 1, 0), pltpu.roll(u, 1, 0))
    u2 = jnp.where(pos < 2, hist, pltpu.roll(u, 2, 0))
    return w[2:3, :] * u + w[1:2, :] * u1 + w[0:1, :] * u2


def _layer_norm(h, g, b):
    hc = h - jnp.mean(h, axis=-1, keepdims=True)
    var = jnp.mean(hc * hc, axis=-1, keepdims=True)
    return hc * lax.rsqrt(var + LN_EPS) * g + b


def _merge_kernel(x_ref, gb_ref, gc_ref, hb_ref, ma_ref, mb_ref, mc_ref, ya_ref, yc_ref,
                  cw_ref, wa_ref, wb_ref, wc_ref, wo_ref, g_ref, b_ref, *rest,
                  sample, t_new, tiles_per_seq, alpha):
    if sample:
        hist_ref, x1_ref, x1b_ref, u_ref = rest
    else:
        x1_ref, x1b_ref, tail_ref, carry_ref = rest

        @pl.when(pl.program_id(0) % tiles_per_seq == 0)
        def _():
            carry_ref[...] = jnp.zeros_like(carry_ref)

    u = gc_ref[...] * hb_ref[...]
    cw = cw_ref[...]
    if sample:
        conv = _conv_sample(u, cw, hist_ref[...], t_new)
        u_ref[...] = u
    else:
        conv = _conv_prompt(u, cw, carry_ref)
        tail_ref[...] = u[u.shape[0] - SUBLANES:, :]
    yb = gb_ref[...] * conv
    mix = (jax.nn.sigmoid(ma_ref[...]) * _dot(ya_ref[...].astype(BF16), wa_ref[...])
           + jax.nn.sigmoid(mb_ref[...]) * _dot(yb.astype(BF16), wb_ref[...])
           + jax.nn.sigmoid(mc_ref[...]) * _dot(yc_ref[...].astype(BF16), wc_ref[...]))
    h = alpha * x_ref[...] + _dot(mix.astype(BF16), wo_ref[...])
    x1 = _layer_norm(h, g_ref[...], b_ref[...])
    x1_ref[...] = x1
    x1b_ref[...] = x1.astype(BF16)


def _merge(x, y, ya, yc, conv_w, wa, wb, wc, wo, ln_g, ln_b, alpha, cols, seq, hist=None,
           t_new=None):
    n, d = x.shape
    w_b = conv_w.shape[1]
    sample = hist is not None
    tm = min(n, MERGE_TM)
    nt = n // tm
    off_b = cols["gate_b"] // w_b
    off_m = cols["m_a"] // d
    row = lambda width, cb: pl.BlockSpec((tm, width), lambda i, cb=cb: (i, cb))
    const = lambda a: pl.BlockSpec(a.shape, lambda i: (0, 0))
    in_specs = [row(d, 0), row(w_b, off_b), row(w_b, off_b + 1), row(w_b, off_b + 2),
                row(d, off_m), row(d, off_m + 1), row(d, off_m + 2),
                row(ya.shape[1], 0), row(yc.shape[1], 0),
                const(conv_w), const(wa), const(wb), const(wc), const(wo), const(ln_g), const(ln_b)]
    args = [x, y, y, y, y, y, y, ya, yc, conv_w, wa, wb, wc, wo, ln_g, ln_b]
    out_specs = [row(d, 0), row(d, 0)]
    out_shape = [jax.ShapeDtypeStruct((n, d), F32), jax.ShapeDtypeStruct((n, d), BF16)]
    scratch = []
    if sample:
        in_specs.append(row(w_b, 0))
        args.append(hist)
        out_specs.append(row(w_b, 0))
        out_shape.append(jax.ShapeDtypeStruct((n, w_b), F32))
    else:
        out_specs.append(pl.BlockSpec((SUBLANES, w_b), lambda i: (i, 0)))
        out_shape.append(jax.ShapeDtypeStruct((nt * SUBLANES, w_b), F32))
        scratch.append(pltpu.VMEM((SUBLANES, w_b), F32))
    kern = functools.partial(_merge_kernel, sample=sample, t_new=t_new,
                             tiles_per_seq=max(seq // tm, 1), alpha=alpha)
    return pl.pallas_call(
        kern, grid=(nt,), in_specs=in_specs, out_specs=out_specs, out_shape=out_shape,
        scratch_shapes=scratch, compiler_params=_params("arbitrary"),
        name="merge_sample" if sample else "merge_prompt",
    )(*args)


def _ffn_kernel(xb_ref, x_ref, wa_ref, wg_ref, cw_ref, cb_ref, wd_ref, g_ref, b_ref, *rest,
                sample, t_new, tiles_per_seq, alpha):
    if sample:
        hist_ref, o_ref, ob_ref, a_ref, acc_ref = rest
    else:
        o_ref, ob_ref, tail_ref, acc_ref, carry_ref = rest
    i = pl.program_id(0)
    f = pl.program_id(1)

    @pl.when(f == 0)
    def _():
        acc_ref[...] = jnp.zeros_like(acc_ref)

    xb = xb_ref[...]
    a = _dot(xb, wa_ref[...])
    gate = _dot(xb, wg_ref[...])
    cw = cw_ref[...]
    if sample:
        conv = _conv_sample(a, cw, hist_ref[...], t_new)
        a_ref[...] = a
    else:
        @pl.when(i % tiles_per_seq == 0)
        def _():
            carry_ref[f] = jnp.zeros(carry_ref.shape[1:], F32)

        conv = _conv_prompt(a, cw, carry_ref.at[f])
        tail_ref[...] = a[a.shape[0] - SUBLANES:, :]
    h = jax.nn.gelu(conv + cb_ref[...]) * gate
    acc_ref[...] += _dot(h.astype(BF16), wd_ref[...])

    @pl.when(f == pl.num_programs(1) - 1)
    def _():
        x2 = _layer_norm(alpha * x_ref[...] + acc_ref[...], g_ref[...], b_ref[...])
        o_ref[...] = x2
        ob_ref[...] = x2.astype(BF16)


def _ffn(xb, x, w_up, conv_w, conv_b, w_down, ln_g, ln_b, alpha, seq, hist=None, t_new=None):
    n, d = x.shape
    d_ff = w_down.shape[0]
    sample = hist is not None
    tm = min(n, FFN_TM)
    tf = 256
    nt = n // tm
    nf = d_ff // tf
    in_specs = [pl.BlockSpec((tm, d), lambda i, f: (i, 0)),
                pl.BlockSpec((tm, d), lambda i, f: (i, 0)),
                pl.BlockSpec((d, tf), lambda i, f: (0, f)),
                pl.BlockSpec((d, tf), lambda i, f: (0, nf + f)),
                pl.BlockSpec((CONV_TAPS, tf), lambda i, f: (0, f)),
                pl.BlockSpec((1, tf), lambda i, f: (0, f)),
                pl.BlockSpec((tf, d), lambda i, f: (f, 0)),
                pl.BlockSpec((1, d), lambda i, f: (0, 0)),
                pl.BlockSpec((1, d), lambda i, f: (0, 0))]
    args = [xb, x, w_up, w_up, conv_w, conv_b, w_down, ln_g, ln_b]
    out_specs = [pl.BlockSpec((tm, d), lambda i, f: (i, 0)),
                 pl.BlockSpec((tm, d), lambda i, f: (i, 0))]
    out_shape = [jax.ShapeDtypeStruct((n, d), F32), jax.ShapeDtypeStruct((n, d), BF16)]
    scratch = [pltpu.VMEM((tm, d), F32)]
    if sample:
        in_specs.append(pl.BlockSpec((tm, tf), lambda i, f: (i, f)))
        args.append(hist)
        out_specs.append(pl.BlockSpec((tm, tf), lambda i, f: (i, f)))
        out_shape.append(jax.ShapeDtypeStruct((n, d_ff), F32))
    else:
        out_specs.append(pl.BlockSpec((SUBLANES, tf), lambda i, f: (i, f)))
        out_shape.append(jax.ShapeDtypeStruct((nt * SUBLANES, d_ff), F32))
        scratch.append(pltpu.VMEM((nf, SUBLANES, tf), F32))
    kern = functools.partial(_ffn_kernel, sample=sample, t_new=t_new,
                             tiles_per_seq=max(seq // tm, 1), alpha=alpha)
    return pl.pallas_call(
        kern, grid=(nt, nf), in_specs=in_specs, out_specs=out_specs, out_shape=out_shape,
        scratch_shapes=scratch, compiler_params=_params("arbitrary", "arbitrary"),
        name="ffn_sample" if sample else "ffn_prompt",
    )(*args)


def _pad_hist(state, t_new):
    n_seq, nb, c = state.shape
    return jnp.pad(state, ((0, 0), (0, t_new - nb), (0, 0))).reshape(n_seq * t_new, c)


def _tails(tail, batch, tiles_per_seq):
    c = tail.shape[1]
    t = tail.reshape(batch, tiles_per_seq, SUBLANES, c)
    return t[:, tiles_per_seq - 1, SUBLANES - 2:, :]


def kernel(x_prompt, x_sample, cache_k, cache_v, state_conv, state_hgrn, state_ffn_conv, page_table,
           w_in, sb_bias, conv_w, hgrn_lb, hgrn_norm_w, w_br_a, w_br_b, w_br_c, w_o, ln1_g, ln1_b,
           w_up, ffn_conv_w, ffn_conv_b, w_down, ln2_g, ln2_b):
    depth, d_model, d_in = w_in.shape
    batch, seq, _ = x_prompt.shape
    n_seq, t_new, _ = x_sample.shape
    n_pool, page, n_heads, d_head = cache_k.shape[1:]
    w_a = n_heads * d_head
    w_b = conv_w.shape[2]
    n_heads_c, dk_c, dv_c = state_hgrn.shape[2:]
    d_ff = w_down.shape[1]
    names = ("q_a", "k_a", "v_a", "gate_b", "gate_c", "h_b", "q_c", "f_c", "i_c", "og_c",
             "m_a", "m_b", "m_c")
    widths = (w_a, w_a, w_a, w_b, w_b, w_b, n_heads_c * dk_c, n_heads_c * dk_c, n_heads_c * dv_c,
              n_heads_c * dv_c, d_model, d_model, d_model)
    sizes = dict(zip(names, widths))
    assert sum(widths) == d_in and dk_c == LANES and dv_c == LANES and 2 * d_head == LANES
    cols_s = _col_starts(names, sizes)
    cols_p = _col_starts([k for k in names if k not in ("k_a", "v_a")], sizes)
    alpha = (2 * depth) ** 0.25

    lbs = jnp.cumsum(jax.nn.softmax(hgrn_lb.astype(F32), axis=0), axis=0)
    lower = lbs - lbs[0]

    ckt = jnp.transpose(cache_k, (0, 1, 3, 4, 2)).reshape(depth, n_pool, w_a, page)
    cvt = jnp.transpose(cache_v, (0, 1, 3, 4, 2)).reshape(depth, n_pool, w_a, page)
    bf = lambda a: a.astype(BF16)
    w_in_b, wa_b, wb_b, wc_b, wo_b, wu_b, wd_b = map(bf, (w_in, w_br_a, w_br_b, w_br_c, w_o, w_up, w_down))
    kv0, kv1 = cols_s["k_a"], cols_s["gate_b"]
    w_rest_b = jnp.concatenate([w_in_b[:, :, :kv0], w_in_b[:, :, kv1:]], axis=2)
    w_kvt_b = jnp.transpose(w_in_b[:, :, kv0:kv1], (0, 2, 1))

    hp = x_prompt.reshape(batch * seq, d_model)
    hs = x_sample.reshape(n_seq * t_new, d_model)
    hpb, hsb = bf(hp), bf(hs)
    outs = [[] for _ in range(10)]
    for l in range(depth):
        row = lambda a: a[l].reshape(1, -1)
        yp = _proj_in(hpb, w_rest_b[l])
        kvt = _proj_kv_t(w_kvt_b[l], hpb, batch, seq)
        ys = _proj_in(hsb, w_in_b[l])
        ya_p = _sb_prompt(yp, kvt, sb_bias[l], batch, seq, n_heads, d_head)
        ya_s = _sb_sample(ys, ckt, cvt, page_table, sb_bias[l], l, n_heads, d_head, t_new)
        yc_p, s_p = _hgrn_prompt(yp, row(lower), row(hgrn_norm_w), batch, seq, n_heads_c, cols_p)
        yc_s, s_s = _hgrn_sample(ys, row(lower), row(hgrn_norm_w), state_hgrn, l, t_new, n_heads_c,
                                 cols_s)
        margs = (conv_w[l], wa_b[l], wb_b[l], wc_b[l], wo_b[l], row(ln1_g), row(ln1_b), alpha)
        x1p, x1pb, tail_b = _merge(hp, yp, ya_p, yc_p, *margs, cols_p, seq)
        x1s, x1sb, u_s = _merge(hs, ys, ya_s, yc_s, *margs, cols_s, t_new,
                                hist=_pad_hist(state_conv[l], t_new), t_new=t_new)
        fargs = (wu_b[l], ffn_conv_w[l], row(ffn_conv_b), wd_b[l], row(ln2_g), row(ln2_b), alpha)
        hp, hpb, tail_f = _ffn(x1pb, x1p, *fargs, seq)
        hs, hsb, a_s = _ffn(x1sb, x1s, *fargs, t_new,
                            hist=_pad_hist(state_ffn_conv[l], t_new), t_new=t_new)

        def heads_t(a):
            return jnp.transpose(a.reshape(batch, n_heads, d_head, seq), (0, 3, 1, 2))

        new = (heads_t(kvt[:, :w_a]), heads_t(kvt[:, w_a:]),
               _tails(tail_b, batch, seq // min(batch * seq, MERGE_TM)),
               s_p,
               _tails(tail_f, batch, seq // min(batch * seq, FFN_TM)),
               ys[:, kv0:kv0 + w_a].reshape(n_seq, t_new, n_heads, d_head),
               ys[:, kv0 + w_a:kv1].reshape(n_seq, t_new, n_heads, d_head),
               u_s.reshape(n_seq, t_new, w_b)[:, t_new - 2:],
               s_s,
               a_s.reshape(n_seq, t_new, d_ff)[:, t_new - 2:])
        for o, v in zip(outs, new):
            o.append(v)

    return (hp.reshape(batch, seq, d_model), hs.reshape(n_seq, t_new, d_model),
            *[jnp.stack(o) for o in outs])
```

```python
import functools
import math

import numpy as np
import jax
import jax.numpy as jnp
from jax import lax
from jax.experimental import pallas as pl
from jax.experimental.pallas import tpu as pltpu

F32 = jnp.float32
BF16 = jnp.bfloat16

LANES = 128
SUBLANES = 8
CONV_TAPS = 3
LN_EPS = 1e-5
RMS_EPS = 1e-6
LOG2E = math.log2(math.e)
HG_CHUNK = 128
MERGE_TM = 256
FFN_TM = 512
FFN_CHUNK = 256
VMEM_LIMIT = 56 * 1024 * 1024


def _params(*sem):
    return pltpu.CompilerParams(dimension_semantics=sem, vmem_limit_bytes=VMEM_LIMIT)


def _split2(x):
    hi = x.astype(BF16)
    lo = (x - hi.astype(F32)).astype(BF16)
    return hi, lo


def _dot(a, b):
    return jnp.dot(a, b, preferred_element_type=F32)


def _dot_nt(a, b):
    return lax.dot_general(a, b, (((1,), (1,)), ((), ())), preferred_element_type=F32)


def _mm_kernel(x_ref, w_ref, o_ref):
    o_ref[...] = _dot(x_ref[...], w_ref[...])


def _proj_in(xb, wb):
    n, k = xb.shape
    m = wb.shape[1]
    tm = min(n, 1024)
    tn = 1024
    return pl.pallas_call(
        _mm_kernel,
        grid=(n // tm, m // tn),
        in_specs=[pl.BlockSpec((tm, k), lambda i, j: (i, 0)),
                  pl.BlockSpec((k, tn), lambda i, j: (0, j))],
        out_specs=pl.BlockSpec((tm, tn), lambda i, j: (i, j)),
        out_shape=jax.ShapeDtypeStruct((n, m), F32),
        compiler_params=_params("parallel", "arbitrary"),
        name="proj_in",
    )(xb, wb)


def _mm_nt_kernel(w_ref, x_ref, *rest):
    k_ref, v_ref = rest[-2:]
    kv = _dot_nt(w_ref[...], x_ref[...])
    half = k_ref.shape[0]
    k_ref[...] = kv[:half]
    v_ref[...] = kv[half:]


def _proj_kv_t(wt, xb, batch, seq, layer, depth, prev=None):
    m, k = wt.shape
    w = m // 2
    tm = min(seq, 1024)
    nt = seq // tm
    out_spec = pl.BlockSpec((None, None, w, tm), lambda b, t: (layer, b, 0, t))
    out_shape = jax.ShapeDtypeStruct((depth, batch, w, seq), F32)
    in_specs = [pl.BlockSpec((m, k), lambda b, t: (0, 0)),
                pl.BlockSpec((tm, k), lambda b, t: (b * nt + t, 0))]
    args, aliases = [wt, xb], {}
    if prev is not None:
        in_specs += [pl.BlockSpec(memory_space=pl.ANY)] * 2
        args += list(prev)
        aliases = {2: 0, 3: 1}
    return pl.pallas_call(
        _mm_nt_kernel,
        grid=(batch, nt),
        in_specs=in_specs,
        out_specs=[out_spec, out_spec],
        out_shape=[out_shape, out_shape],
        input_output_aliases=aliases,
        compiler_params=_params("parallel", "arbitrary"),
        name="proj_kv_t",
    )(*args)


def _softplus2(z):
    neg_abs = lax.bitcast_convert_type(
        lax.bitcast_convert_type(z, jnp.uint32) | jnp.uint32(0x80000000), F32)
    return jnp.maximum(z, 0.0) + jnp.log2(1.0 + jnp.exp2(neg_abs))


def _suffix_sums(sp, tri2):
    n = sp.shape[1]
    hi, lo = _split2(sp)
    c = _dot(jnp.concatenate([hi, lo], axis=1), tri2)
    return c[:, :n], c[:, n:]


def _tri_suffix2(n):
    j = np.arange(n)[:, None]
    s = np.arange(n)[None, :]
    ext = np.concatenate([(j >= s).astype(np.float32), np.ones((n, LANES), np.float32)], axis=1)
    return jnp.asarray(np.concatenate([ext, ext], axis=0), dtype=BF16)


def _sb_prompt_kernel(bias_ref, q_ref, kt_ref, vt_ref, tri_ref, o_ref, kb_ref, vb_ref, carry_ref,
                      acc_ref, *, tq, c2):
    hp = pl.program_id(1)
    qi = pl.program_id(2)
    n_blk = kb_ref.shape[0]
    sub = tri_ref.shape[0] // 2

    @pl.when(qi == 0)
    def _():
        for j in range(n_blk):
            kb_ref[j] = kt_ref[:, j * tq:(j + 1) * tq].astype(BF16)
            vb_ref[j] = vt_ref[:, j * tq:(j + 1) * tq].astype(BF16)

    lane = lax.broadcasted_iota(jnp.int32, (1, LANES), 1)
    half = LANES // 2
    q = q_ref[...] * c2
    qh = [jnp.where((lane >= h * half) & (lane < (h + 1) * half), q, 0.0).astype(BF16)
          for h in range(2)]
    bias2 = [bias_ref[2 * hp + h] * LOG2E for h in range(2)]
    tri2 = tri_ref[...]
    carry_ref[...] = jnp.zeros_like(carry_ref)
    acc_ref[...] = jnp.zeros_like(acc_ref)

    def span(j, mask):
        kb = kb_ref[j]
        vb = vb_ref[j]
        for h in range(2):
            z = _dot(qh[h], kb) + bias2[h]
            sp = _softplus2(z)
            if mask is not None:
                sp = jnp.where(mask, sp, 0.0)
            sums = [_suffix_sums(sp[:, i * sub:(i + 1) * sub], tri2) for i in range(tq // sub)]
            carry = carry_ref[h]
            parts = []
            for i in reversed(range(tq // sub)):
                c, tot = sums[i]
                parts.append(z[:, i * sub:(i + 1) * sub] - c - carry)
                carry = carry + tot
            carry_ref[h] = carry
            w = jnp.exp2(jnp.concatenate(parts[::-1], axis=1))
            if mask is not None:
                w = jnp.where(mask, w, 0.0)
            acc_ref[h] += _dot_nt(w.astype(BF16), vb)

    row = lax.broadcasted_iota(jnp.int32, (tq, tq), 0)
    col = lax.broadcasted_iota(jnp.int32, (tq, tq), 1)
    span(qi, col < row)

    def body(i, _):
        span(qi - 1 - i, None)
        return 0

    lax.fori_loop(0, qi, body, 0)
    o_ref[...] = jnp.where(lane < half, acc_ref[0], acc_ref[1])


def _sb_prompt(q_src, kt, vt, layer, sb_bias, batch, seq, n_heads, d_head):
    n = q_src.shape[0]
    w_a = n_heads * d_head
    tq = 512
    nq = seq // tq
    npair = w_a // LANES
    tri2 = _tri_suffix2(LANES)
    kern = functools.partial(_sb_prompt_kernel, tq=tq, c2=LOG2E * d_head ** -0.5)
    return pl.pallas_call(
        kern,
        grid_spec=pltpu.PrefetchScalarGridSpec(
            num_scalar_prefetch=1,
            grid=(batch, npair, nq),
            in_specs=[
                pl.BlockSpec((tq, LANES), lambda b, hp, qi, bias: (b * nq + qi, hp)),
                pl.BlockSpec((None, None, LANES, seq), lambda b, hp, qi, bias: (layer, b, hp, 0)),
                pl.BlockSpec((None, None, LANES, seq), lambda b, hp, qi, bias: (layer, b, hp, 0)),
                pl.BlockSpec(tri2.shape, lambda b, hp, qi, bias: (0, 0)),
            ],
            out_specs=pl.BlockSpec((tq, LANES), lambda b, hp, qi, bias: (b * nq + qi, hp)),
            scratch_shapes=[pltpu.VMEM((nq, LANES, tq), BF16), pltpu.VMEM((nq, LANES, tq), BF16),
                            pltpu.VMEM((2, tq, LANES), F32), pltpu.VMEM((2, tq, LANES), F32)],
        ),
        out_shape=jax.ShapeDtypeStruct((n, w_a), F32),
        compiler_params=_params("parallel", "parallel", "arbitrary"),
        name="sb_prompt",
    )(sb_bias, q_src, kt, vt, tri2)


def _sb_sample_kernel(pt_ref, q_ref, kn_ref, vn_ref, bias_ref, tri_ref, *rest,
                      n_pages, n_heads, d_head, t_new, c2):
    kpages = rest[:n_pages]
    vpages = rest[n_pages:2 * n_pages]
    o_ref = rest[2 * n_pages]
    rows = n_heads * t_new
    w_a = n_heads * d_head
    page = tri_ref.shape[0] // 2
    head_mask = (lax.broadcasted_iota(jnp.int32, (rows, w_a), 0) // t_new
                 == lax.broadcasted_iota(jnp.int32, (rows, w_a), 1) // d_head)
    bias2 = bias_ref[...] * LOG2E
    tri2 = tri_ref[...]

    q = q_ref[...] * c2
    qbd = jnp.where(head_mask, jnp.concatenate([q] * n_heads, axis=0), 0.0).astype(BF16)

    pad = jnp.zeros((page - t_new, w_a), F32)
    kn = jnp.concatenate([kn_ref[...], pad], axis=0).astype(BF16)
    vn = jnp.concatenate([vn_ref[...], pad], axis=0).astype(BF16)
    mask = (lax.broadcasted_iota(jnp.int32, (rows, page), 1)
            < lax.broadcasted_iota(jnp.int32, (rows, page), 0) % t_new)
    zs = [_dot_nt(qbd, kn) + bias2]
    zs += [_dot(qbd, kpages[r][...].astype(BF16)) + bias2 for r in range(n_pages)]
    sps = [jnp.where(mask, _softplus2(zs[0]), 0.0)] + [_softplus2(z) for z in zs[1:]]
    c_all = _dot(jnp.concatenate([jnp.concatenate(_split2(sp), axis=1) for sp in sps], axis=0), tri2)
    carry = jnp.zeros((rows, LANES), F32)
    acc = jnp.zeros((rows, w_a), F32)
    for r in range(n_pages + 1):
        c = c_all[r * rows:(r + 1) * rows]
        w = jnp.exp2(zs[r] - c[:, :page] - carry)
        carry = carry + c[:, page:]
        if r == 0:
            acc = acc + _dot(jnp.where(mask, w, 0.0).astype(BF16), vn)
        else:
            acc = acc + _dot_nt(w.astype(BF16), vpages[r - 1][...].astype(BF16))
    a = jnp.where(head_mask, acc, 0.0)
    out = a[0:t_new]
    for h in range(1, n_heads):
        out = out + a[h * t_new:(h + 1) * t_new]
    o_ref[...] = out


def _sb_sample(ys, ckt, cvt, page_table, sb_bias, layer, n_heads, d_head, t_new):
    n = ys.shape[0]
    n_seq, n_pages = page_table.shape
    page = ckt.shape[3]
    w_a = n_heads * d_head
    rows = n_heads * t_new
    bias_rep = jnp.broadcast_to(jnp.repeat(sb_bias.astype(F32), t_new)[:, None], (rows, LANES))
    tri2 = _tri_suffix2(page)

    def page_spec(r):
        return pl.BlockSpec((None, None, w_a, page),
                            lambda b, pt, r=r: (layer, pt[b, n_pages - 1 - r], 0, 0))

    kern = functools.partial(_sb_sample_kernel, n_pages=n_pages, n_heads=n_heads, d_head=d_head,
                             t_new=t_new, c2=LOG2E * d_head ** -0.5)
    return pl.pallas_call(
        kern,
        grid_spec=pltpu.PrefetchScalarGridSpec(
            num_scalar_prefetch=1,
            grid=(n_seq,),
            in_specs=[
                pl.BlockSpec((t_new, w_a), lambda b, pt: (b, 0)),
                pl.BlockSpec((t_new, w_a), lambda b, pt: (b, 1)),
                pl.BlockSpec((t_new, w_a), lambda b, pt: (b, 2)),
                pl.BlockSpec((rows, LANES), lambda b, pt: (0, 0)),
                pl.BlockSpec(tri2.shape, lambda b, pt: (0, 0)),
            ] + [page_spec(r) for r in range(n_pages)] * 2,
            out_specs=pl.BlockSpec((t_new, w_a), lambda b, pt: (b, 0)),
        ),
        out_shape=jax.ShapeDtypeStruct((n, w_a), F32),
        compiler_params=_params("arbitrary"),
        name="sb_sample",
    )(page_table, ys, ys, ys, bias_rep, tri2, *([ckt] * n_pages), *([cvt] * n_pages))


def _hgrn_masks(c, seg):
    t = np.arange(c)[:, None]
    j = np.arange(c)[None, :]
    same = (t // seg) == (j // seg)
    halves = []
    h = seg // 2
    while h >= 1:
        halves.append(h)
        h //= 2
    blocks = [same & (j <= t), same & (j > t)]
    ublocks, lblocks, pairs = [], [], []
    for h in halves:
        upper_t = ((t // h) % 2) == 1
        m_t = (t // h) * h
        ublocks.append(upper_t & (j >= m_t) & (j <= t))
        lower_t = ~upper_t
        m_next = (t // h + 1) * h
        lblocks.append(lower_t & (j > t) & (j < m_next))
        s = j
        pairs.append(upper_t & ((((s // h) % 2) == 0)) & ((t // (2 * h)) == (s // (2 * h))))
    pairs.append(t == j)
    m_all = np.concatenate(blocks + ublocks + lblocks, axis=0).astype(np.float32)
    m_cat = np.concatenate([m_all, m_all], axis=1)
    return (jnp.asarray(m_cat, dtype=BF16), jnp.asarray(np.stack(pairs).astype(np.float32)),
            len(halves))


def _hgrn_gates(zf, lb):
    log_sig = jnp.minimum(zf, 0.0) - jnp.log(1.0 + jnp.exp(-jnp.abs(zf)))
    a1 = jnp.log(lb)
    a2 = jnp.log(1.0 - lb) + log_sig
    g = jnp.maximum(a1, a2) + jnp.log(1.0 + jnp.exp(-jnp.abs(a1 - a2)))
    return g, (1.0 - lb) * jax.nn.sigmoid(-zf)


def _hgrn_exponents(gs, mcat):
    cols = [jnp.concatenate(_split2(g), axis=0) for g in gs]
    return _dot(mcat, cols[0] if len(cols) == 1 else jnp.concatenate(cols, axis=1))


def _hgrn_intra(q, k, v, blk, pairs, n_levels):
    vb = v.astype(BF16)
    scores = _dot_nt(q.astype(BF16), k.astype(BF16)) * pairs[n_levels]
    for i in range(n_levels):
        qe = (q * jnp.exp(blk(2 + i))).astype(BF16)
        ke = (k * jnp.exp(blk(2 + n_levels + i))).astype(BF16)
        scores = scores + _dot_nt(qe, ke) * pairs[i]
    eb = jnp.exp(blk(0))
    kd = k * jnp.exp(blk(1))
    return _dot(scores.astype(BF16), vb), (q * eb).astype(BF16), eb.T, kd.T.astype(BF16), vb


def _hgrn_finish(o, og, norm_w):
    o = o * lax.rsqrt(jnp.mean(o * o, axis=-1, keepdims=True) + RMS_EPS) * norm_w
    return o * (og * jax.nn.sigmoid(og))


def _hgrn_prompt_kernel(q_ref, f_ref, v_ref, og_ref, lb_ref, nw_ref, mcat_ref, pairs_ref,
                        y_ref, s_out_ref, s_ref, *, n_levels, chunk):
    ti = pl.program_id(2)

    @pl.when(ti == 0)
    def _():
        s_ref[...] = jnp.zeros_like(s_ref)

    pairs = pairs_ref[...]
    n_chunks = q_ref.shape[0] // chunk
    rows = lambda a, r: a[r * chunk:(r + 1) * chunk]
    g, k = _hgrn_gates(f_ref[...], lb_ref[...])
    e = _hgrn_exponents([rows(g, r) for r in range(n_chunks)], mcat_ref[...])
    q = q_ref[...]
    v = v_ref[...]
    intra = [_hgrn_intra(rows(q, r), rows(k, r), rows(v, r),
                         lambda i, r=r: e[i * chunk:(i + 1) * chunk, r * LANES:(r + 1) * LANES],
                         pairs, n_levels) for r in range(n_chunks)]
    s = s_ref[...]
    for r, (o, qe, eb_t, kd_t, vb) in enumerate(intra):
        o = o + _dot(qe, s.astype(BF16))
        s = s * eb_t[:, chunk - 1:chunk] + _dot(kd_t, vb)
        sl = pl.ds(r * chunk, chunk)
        y_ref[sl, :] = _hgrn_finish(o, og_ref[sl, :], nw_ref[...])
    s_ref[...] = s

    @pl.when(ti == pl.num_programs(2) - 1)
    def _():
        s_out_ref[...] = s


def _hgrn_sample_kernel(q_ref, f_ref, v_ref, og_ref, lb_ref, nw_ref, mcat_ref, pairs_ref,
                        s0_ref, *rest, n_levels, seg):
    y_ref, s_out_ref = rest[-2:]
    chunk = q_ref.shape[0]
    n_seq = chunk // seg
    lane_seq = lax.broadcasted_iota(jnp.int32, (1, chunk), 1) // seg
    row_seq = lax.broadcasted_iota(jnp.int32, (chunk, 1), 0) // seg
    g, k = _hgrn_gates(f_ref[...], lb_ref[...])
    e = _hgrn_exponents([g], mcat_ref[...])
    o, qe, eb_t, kd_t, vb = _hgrn_intra(q_ref[...], k, v_ref[...],
                                        lambda i: e[i * chunk:(i + 1) * chunk],
                                        pairs_ref[...], n_levels)
    for i in range(n_seq):
        s0 = s0_ref[i]
        o = o + _dot(jnp.where(row_seq == i, qe, jnp.zeros_like(qe)), s0.astype(BF16))
        last = (i + 1) * seg - 1
        kd_i = jnp.where(lane_seq == i, kd_t, jnp.zeros_like(kd_t))
        s_out_ref[i] = s0 * eb_t[:, last:last + 1] + _dot(kd_i, vb)
    y_ref[...] = _hgrn_finish(o, og_ref[...], nw_ref[...])


def _col_starts(names, sizes):
    starts, off = {}, 0
    for name in names:
        starts[name] = off
        off += sizes[name]
    return starts


def _hgrn_col_blocks(cols):
    return tuple(cols[k] // LANES for k in ("q_c", "f_c", "i_c", "og_c"))


def _hgrn_prompt(y, lower, norm_w, batch, seq, n_heads_c, cols):
    n = y.shape[0]
    chunk = HG_CHUNK
    tt = 512
    nt = seq // tt
    mcat, pairs, n_levels = _hgrn_masks(chunk, chunk)
    cq, cf, ci, cg = _hgrn_col_blocks(cols)

    def col(cb):
        return pl.BlockSpec((tt, LANES), lambda b, h, t, cb=cb: (b * nt + t, cb + h))

    const2 = lambda shape: pl.BlockSpec(shape, lambda b, h, t: (0, 0))
    kern = functools.partial(_hgrn_prompt_kernel, n_levels=n_levels, chunk=chunk)
    return pl.pallas_call(
        kern,
        grid=(batch, n_heads_c, nt),
        in_specs=[col(cq), col(cf), col(ci), col(cg),
                  pl.BlockSpec((1, LANES), lambda b, h, t: (0, h)),
                  const2((1, LANES)), const2(mcat.shape),
                  pl.BlockSpec(pairs.shape, lambda b, h, t: (0, 0, 0))],
        out_specs=[pl.BlockSpec((tt, LANES), lambda b, h, t: (b * nt + t, h)),
                   pl.BlockSpec((None, None, LANES, LANES), lambda b, h, t: (b, h, 0, 0))],
        out_shape=[jax.ShapeDtypeStruct((n, n_heads_c * LANES), F32),
                   jax.ShapeDtypeStruct((batch, n_heads_c, LANES, LANES), F32)],
        scratch_shapes=[pltpu.VMEM((LANES, LANES), F32)],
        compiler_params=_params("parallel", "parallel", "arbitrary"),
        name="hgrn_prompt",
    )(y, y, y, y, lower, norm_w, mcat, pairs)


def _hgrn_sample(ys, lower, norm_w, state, layer, t_new, n_heads_c, cols, prev=None):
    n = ys.shape[0]
    chunk = HG_CHUNK
    g = chunk // t_new
    mcat, pairs, n_levels = _hgrn_masks(chunk, t_new)
    cq, cf, ci, cg = _hgrn_col_blocks(cols)

    def col(cb):
        return pl.BlockSpec((chunk, LANES), lambda i, h, cb=cb: (i, cb + h))

    const2 = lambda shape: pl.BlockSpec(shape, lambda i, h: (0, 0))
    state_spec = pl.BlockSpec((None, g, None, LANES, LANES), lambda i, h: (layer, i, h, 0, 0))
    in_specs = [col(cq), col(cf), col(ci), col(cg),
                pl.BlockSpec((1, LANES), lambda i, h: (0, h)),
                const2((1, LANES)), const2(mcat.shape),
                pl.BlockSpec(pairs.shape, lambda i, h: (0, 0, 0)), state_spec]
    args, aliases = [ys, ys, ys, ys, lower, norm_w, mcat, pairs, state], {}
    if prev is not None:
        in_specs.append(pl.BlockSpec(memory_space=pl.ANY))
        args.append(prev)
        aliases = {len(args) - 1: 1}
    kern = functools.partial(_hgrn_sample_kernel, n_levels=n_levels, seg=t_new)
    return pl.pallas_call(
        kern,
        grid=(n // chunk, n_heads_c),
        in_specs=in_specs,
        out_specs=[pl.BlockSpec((chunk, LANES), lambda i, h: (i, h)), state_spec],
        out_shape=[jax.ShapeDtypeStruct((n, n_heads_c * LANES), F32),
                   jax.ShapeDtypeStruct(state.shape, F32)],
        input_output_aliases=aliases,
        compiler_params=_params("parallel", "parallel"),
        name="hgrn_sample",
    )(*args)


def _conv_prompt(u, w, carry_ref):
    tm = u.shape[0]
    r = lax.broadcasted_iota(jnp.int32, (tm, 1), 0)
    c6 = carry_ref[SUBLANES - 2:SUBLANES - 1, :]
    c7 = carry_ref[SUBLANES - 1:SUBLANES, :]
    u1 = jnp.where(r == 0, c7, pltpu.roll(u, 1, 0))
    u2 = jnp.where(r == 0, c6, jnp.where(r == 1, c7, pltpu.roll(u, 2, 0)))
    carry_ref[...] = u[tm - SUBLANES:, :]
    return w[2:3, :] * u + w[1:2, :] * u1 + w[0:1, :] * u2


def _conv_sample(u, w, hist, t_new):
    tm = u.shape[0]
    pos = lax.broadcasted_iota(jnp.int32, (tm, 1), 0) % t_new
    u1 = jnp.where(pos == 0, pltpu.roll(hist, tm - 1, 0), pltpu.roll(u, 1, 0))
    u2 = jnp.where(pos < 2, hist, pltpu.roll(u, 2, 0))
    return w[2:3, :] * u + w[1:2, :] * u1 + w[0:1, :] * u2


def _layer_norm(h, g, b):
    hc = h - jnp.mean(h, axis=-1, keepdims=True)
    var = jnp.mean(hc * hc, axis=-1, keepdims=True)
    return hc * lax.rsqrt(var + LN_EPS) * g + b


def _merge_kernel(x_ref, gb_ref, gc_ref, hb_ref, ma_ref, mb_ref, mc_ref, ya_ref, yc_ref,
                  cw_ref, wa_ref, wb_ref, wc_ref, wo_ref, g_ref, b_ref, *rest,
                  sample, t_new, tiles_per_seq, alpha):
    if sample:
        hist_ref, x1_ref, x1b_ref, u_ref = rest
    else:
        x1_ref, x1b_ref, tail_ref, carry_ref = rest

        @pl.when(pl.program_id(0) % tiles_per_seq == 0)
        def _():
            carry_ref[...] = jnp.zeros_like(carry_ref)

    u = gc_ref[...] * hb_ref[...]
    cw = cw_ref[...]
    if sample:
        conv = _conv_sample(u, cw, hist_ref[...], t_new)
        u_ref[...] = u
    else:
        conv = _conv_prompt(u, cw, carry_ref)
        tail_ref[...] = u[u.shape[0] - SUBLANES:, :]
    yb = gb_ref[...] * conv
    mix = (jax.nn.sigmoid(ma_ref[...]) * _dot(ya_ref[...].astype(BF16), wa_ref[...])
           + jax.nn.sigmoid(mb_ref[...]) * _dot(yb.astype(BF16), wb_ref[...])
           + jax.nn.sigmoid(mc_ref[...]) * _dot(yc_ref[...].astype(BF16), wc_ref[...]))
    h = alpha * x_ref[...] + _dot(mix.astype(BF16), wo_ref[...])
    x1 = _layer_norm(h, g_ref[...], b_ref[...])
    x1_ref[...] = x1
    x1b_ref[...] = x1.astype(BF16)


def _merge(x, y, ya, yc, conv_w, wa, wb, wc, wo, ln_g, ln_b, alpha, cols, seq, hist=None,
           t_new=None):
    n, d = x.shape
    w_b = conv_w.shape[1]
    sample = hist is not None
    tm = min(n, MERGE_TM)
    nt = n // tm
    off_b = cols["gate_b"] // w_b
    off_m = cols["m_a"] // d
    row = lambda width, cb: pl.BlockSpec((tm, width), lambda i, cb=cb: (i, cb))
    const = lambda a: pl.BlockSpec(a.shape, lambda i: (0, 0))
    in_specs = [row(d, 0), row(w_b, off_b), row(w_b, off_b + 1), row(w_b, off_b + 2),
                row(d, off_m), row(d, off_m + 1), row(d, off_m + 2),
                row(ya.shape[1], 0), row(yc.shape[1], 0),
                const(conv_w), const(wa), const(wb), const(wc), const(wo), const(ln_g), const(ln_b)]
    args = [x, y, y, y, y, y, y, ya, yc, conv_w, wa, wb, wc, wo, ln_g, ln_b]
    out_specs = [row(d, 0), row(d, 0)]
    out_shape = [jax.ShapeDtypeStruct((n, d), F32), jax.ShapeDtypeStruct((n, d), BF16)]
    scratch = []
    if sample:
        in_specs.append(row(w_b, 0))
        args.append(hist)
        out_specs.append(row(w_b, 0))
        out_shape.append(jax.ShapeDtypeStruct((n, w_b), F32))
    else:
        out_specs.append(pl.BlockSpec((SUBLANES, w_b), lambda i: (i, 0)))
        out_shape.append(jax.ShapeDtypeStruct((nt * SUBLANES, w_b), F32))
        scratch.append(pltpu.VMEM((SUBLANES, w_b), F32))
    kern = functools.partial(_merge_kernel, sample=sample, t_new=t_new,
                             tiles_per_seq=max(seq // tm, 1), alpha=alpha)
    return pl.pallas_call(
        kern, grid=(nt,), in_specs=in_specs, out_specs=out_specs, out_shape=out_shape,
        scratch_shapes=scratch, compiler_params=_params("arbitrary"),
        name="merge_sample" if sample else "merge_prompt",
    )(*args)


def _ffn_kernel(xb_ref, x_ref, wu_ref, cw_ref, cb_ref, wd_ref, g_ref, b_ref, *rest,
                sample, t_new, tiles_per_seq, alpha):
    if sample:
        hist_ref, o_ref, ob_ref, a_ref = rest
    else:
        o_ref, ob_ref, tail_ref, carry_ref = rest
    d_ff = wd_ref.shape[0]
    if not sample:
        @pl.when(pl.program_id(0) % tiles_per_seq == 0)
        def _():
            carry_ref[...] = jnp.zeros_like(carry_ref)

    xb = xb_ref[...]
    y = jnp.zeros(x_ref.shape, F32)
    for c in range(d_ff // FFN_CHUNK):
        lo, hi = c * FFN_CHUNK, (c + 1) * FFN_CHUNK
        a = _dot(xb, wu_ref[:, lo:hi])
        gate = _dot(xb, wu_ref[:, d_ff + lo:d_ff + hi])
        cw = cw_ref[:, lo:hi]
        if sample:
            conv = _conv_sample(a, cw, hist_ref[:, lo:hi], t_new)
            a_ref[:, lo:hi] = a
        else:
            conv = _conv_prompt(a, cw, carry_ref.at[:, lo:hi])
            tail_ref[:, lo:hi] = a[a.shape[0] - SUBLANES:, :]
        h = jax.nn.gelu(conv + cb_ref[:, lo:hi]) * gate
        y = y + _dot(h.astype(BF16), wd_ref[lo:hi, :])
    x2 = _layer_norm(alpha * x_ref[...] + y, g_ref[...], b_ref[...])
    o_ref[...] = x2
    ob_ref[...] = x2.astype(BF16)


def _ffn(xb, x, w_up, conv_w, conv_b, w_down, ln_g, ln_b, alpha, seq, hist=None, t_new=None):
    n, d = x.shape
    d_ff = w_down.shape[0]
    sample = hist is not None
    tm = min(n, FFN_TM // 2 if sample else FFN_TM)
    nt = n // tm
    row = lambda width: pl.BlockSpec((tm, width), lambda i: (i, 0))
    const = lambda a: pl.BlockSpec(a.shape, lambda i: (0, 0), pipeline_mode=pl.Buffered(1))
    in_specs = [row(d), row(d), const(w_up), const(conv_w), const(conv_b), const(w_down),
                const(ln_g), const(ln_b)]
    args = [xb, x, w_up, conv_w, conv_b, w_down, ln_g, ln_b]
    out_specs = [row(d), row(d)]
    out_shape = [jax.ShapeDtypeStruct((n, d), F32), jax.ShapeDtypeStruct((n, d), BF16)]
    scratch = []
    if sample:
        in_specs.append(row(d_ff))
        args.append(hist)
        out_specs.append(row(d_ff))
        out_shape.append(jax.ShapeDtypeStruct((n, d_ff), F32))
    else:
        out_specs.append(pl.BlockSpec((SUBLANES, d_ff), lambda i: (i, 0)))
        out_shape.append(jax.ShapeDtypeStruct((nt * SUBLANES, d_ff), F32))
        scratch.append(pltpu.VMEM((SUBLANES, d_ff), F32))
    kern = functools.partial(_ffn_kernel, sample=sample, t_new=t_new,
                             tiles_per_seq=max(seq // tm, 1), alpha=alpha)
    return pl.pallas_call(
        kern, grid=(nt,), in_specs=in_specs, out_specs=out_specs, out_shape=out_shape,
        scratch_shapes=scratch, compiler_params=_params("arbitrary"),
        name="ffn_sample" if sample else "ffn_prompt",
    )(*args)


def _pad_hist(state, t_new):
    n_seq, nb, c = state.shape
    return jnp.pad(state, ((0, 0), (0, t_new - nb), (0, 0))).reshape(n_seq * t_new, c)


def _tails(tail, batch, tiles_per_seq):
    c = tail.shape[1]
    t = tail.reshape(batch, tiles_per_seq, SUBLANES, c)
    return t[:, tiles_per_seq - 1, SUBLANES - 2:, :]


def kernel(x_prompt, x_sample, cache_k, cache_v, state_conv, state_hgrn, state_ffn_conv, page_table,
           w_in, sb_bias, conv_w, hgrn_lb, hgrn_norm_w, w_br_a, w_br_b, w_br_c, w_o, ln1_g, ln1_b,
           w_up, ffn_conv_w, ffn_conv_b, w_down, ln2_g, ln2_b):
    depth, d_model, d_in = w_in.shape
    batch, seq, _ = x_prompt.shape
    n_seq, t_new, _ = x_sample.shape
    n_pool, page, n_heads, d_head = cache_k.shape[1:]
    w_a = n_heads * d_head
    w_b = conv_w.shape[2]
    n_heads_c, dk_c, dv_c = state_hgrn.shape[2:]
    d_ff = w_down.shape[1]
    names = ("q_a", "k_a", "v_a", "gate_b", "gate_c", "h_b", "q_c", "f_c", "i_c", "og_c",
             "m_a", "m_b", "m_c")
    widths = (w_a, w_a, w_a, w_b, w_b, w_b, n_heads_c * dk_c, n_heads_c * dk_c, n_heads_c * dv_c,
              n_heads_c * dv_c, d_model, d_model, d_model)
    sizes = dict(zip(names, widths))
    assert sum(widths) == d_in and dk_c == LANES and dv_c == LANES and 2 * d_head == LANES
    cols_s = _col_starts(names, sizes)
    cols_p = _col_starts([k for k in names if k not in ("k_a", "v_a")], sizes)
    alpha = (2 * depth) ** 0.25

    lbs = jnp.cumsum(jax.nn.softmax(hgrn_lb.astype(F32), axis=0), axis=0)
    lower = lbs - lbs[0]

    ckt = jnp.transpose(cache_k, (0, 1, 3, 4, 2)).reshape(depth, n_pool, w_a, page)
    cvt = jnp.transpose(cache_v, (0, 1, 3, 4, 2)).reshape(depth, n_pool, w_a, page)
    bf = lambda a: a.astype(BF16)
    w_in_b, wa_b, wb_b, wc_b, wo_b, wu_b, wd_b = map(bf, (w_in, w_br_a, w_br_b, w_br_c, w_o, w_up, w_down))
    kv0, kv1 = cols_s["k_a"], cols_s["gate_b"]
    w_rest_b = jnp.concatenate([w_in_b[:, :, :kv0], w_in_b[:, :, kv1:]], axis=2)
    w_kvt_b = jnp.transpose(w_in_b[:, :, kv0:kv1], (0, 2, 1))

    hp = x_prompt.reshape(batch * seq, d_model)
    hs = x_sample.reshape(n_seq * t_new, d_model)
    hpb, hsb = bf(hp), bf(hs)
    outs = {k: [] for k in ("conv_p", "s_p", "ffn_p", "k_s", "v_s", "conv_s", "ffn_s")}
    kv_all = None
    s_all = None
    for l in range(depth):
        row = lambda a: a[l].reshape(1, -1)
        yp = _proj_in(hpb, w_rest_b[l])
        kv_all = _proj_kv_t(w_kvt_b[l], hpb, batch, seq, l, depth, kv_all)
        ys = _proj_in(hsb, w_in_b[l])
        ya_p = _sb_prompt(yp, *kv_all, l, sb_bias[l], batch, seq, n_heads, d_head)
        ya_s = _sb_sample(ys, ckt, cvt, page_table, sb_bias[l], l, n_heads, d_head, t_new)
        yc_p, s_p = _hgrn_prompt(yp, row(lower), row(hgrn_norm_w), batch, seq, n_heads_c, cols_p)
        yc_s, s_all = _hgrn_sample(ys, row(lower), row(hgrn_norm_w), state_hgrn, l, t_new, n_heads_c,
                                   cols_s, s_all)
        margs = (conv_w[l], wa_b[l], wb_b[l], wc_b[l], wo_b[l], row(ln1_g), row(ln1_b), alpha)
        x1p, x1pb, tail_b = _merge(hp, yp, ya_p, yc_p, *margs, cols_p, seq)
        x1s, x1sb, u_s = _merge(hs, ys, ya_s, yc_s, *margs, cols_s, t_new,
                                hist=_pad_hist(state_conv[l], t_new), t_new=t_new)
        fargs = (wu_b[l], ffn_conv_w[l], row(ffn_conv_b), wd_b[l], row(ln2_g), row(ln2_b), alpha)
        hp, hpb, tail_f = _ffn(x1pb, x1p, *fargs, seq)
        hs, hsb, a_s = _ffn(x1sb, x1s, *fargs, t_new,
                            hist=_pad_hist(state_ffn_conv[l], t_new), t_new=t_new)

        outs["conv_p"].append(_tails(tail_b, batch, seq // min(batch * seq, MERGE_TM)))
        outs["s_p"].append(s_p)
        outs["ffn_p"].append(_tails(tail_f, batch, seq // min(batch * seq, FFN_TM)))
        outs["k_s"].append(ys[:, kv0:kv0 + w_a].reshape(n_seq, t_new, n_heads, d_head))
        outs["v_s"].append(ys[:, kv0 + w_a:kv1].reshape(n_seq, t_new, n_heads, d_head))
        outs["conv_s"].append(u_s.reshape(n_seq, t_new, w_b)[:, t_new - 2:])
        outs["ffn_s"].append(a_s.reshape(n_seq, t_new, d_ff)[:, t_new - 2:])

    def heads_t(a):
        return jnp.transpose(a.reshape(depth, batch, n_heads, d_head, seq), (0, 1, 4, 2, 3))

    st = {k: jnp.stack(v) for k, v in outs.items()}
    return (hp.reshape(batch, seq, d_model), hs.reshape(n_seq, t_new, d_model),
            heads_t(kv_all[0]), heads_t(kv_all[1]), st["conv_p"], st["s_p"], st["ffn_p"],
            st["k_s"], st["v_s"], st["conv_s"], s_all, st["ffn_s"])
```

```python
import functools
import math

import numpy as np
import jax
import jax.numpy as jnp
from jax import lax
from jax.experimental import pallas as pl
from jax.experimental.pallas import tpu as pltpu

F32 = jnp.float32
BF16 = jnp.bfloat16

LANES = 128
SUBLANES = 8
CONV_TAPS = 3
LN_EPS = 1e-5
RMS_EPS = 1e-6
LOG2E = math.log2(math.e)
HG_CHUNK = 128
MERGE_TM = 256
FFN_TM = 512
SB_TQ = 512
SB_DIAG_GROUPS = 1
FFN_CHUNK = 2816
VMEM_LIMIT = 56 * 1024 * 1024


def _params(*sem):
    return pltpu.CompilerParams(dimension_semantics=sem, vmem_limit_bytes=VMEM_LIMIT)


def _split2(x):
    hi = x.astype(BF16)
    lo = (x - hi.astype(F32)).astype(BF16)
    return hi, lo


def _dot(a, b):
    return jnp.dot(a, b, preferred_element_type=F32)


def _dot_nt(a, b):
    return lax.dot_general(a, b, (((1,), (1,)), ((), ())), preferred_element_type=F32)


def _mm_kernel(x_ref, w_ref, o_ref):
    o_ref[...] = _dot(x_ref[...], w_ref[...])


def _proj_in(xb, wb):
    n, k = xb.shape
    m = wb.shape[1]
    tm = min(n, 1024)
    tn = 1024
    return pl.pallas_call(
        _mm_kernel,
        grid=(n // tm, m // tn),
        in_specs=[pl.BlockSpec((tm, k), lambda i, j: (i, 0)),
                  pl.BlockSpec((k, tn), lambda i, j: (0, j))],
        out_specs=pl.BlockSpec((tm, tn), lambda i, j: (i, j)),
        out_shape=jax.ShapeDtypeStruct((n, m), F32),
        compiler_params=_params("parallel", "arbitrary"),
        name="proj_in",
    )(xb, wb)


def _mm_mixed_kernel(x_ref, w_ref, o_ref, f_ref, *, jf, off):
    acc = _dot(x_ref[...], w_ref[...])
    o_ref[...] = acc.astype(BF16)

    @pl.when(pl.program_id(1) == jf)
    def _():
        f_ref[...] = acc[:, off:off + f_ref.shape[1]]


def _proj_in_mixed(xb, wb, f32_start, f32_width):
    n, k = xb.shape
    m = wb.shape[1]
    tm = min(n, 1024)
    tn = 1024
    jf, off = divmod(f32_start, tn)
    assert off + f32_width <= tn
    return pl.pallas_call(
        functools.partial(_mm_mixed_kernel, jf=jf, off=off),
        grid=(n // tm, m // tn),
        in_specs=[pl.BlockSpec((tm, k), lambda i, j: (i, 0)),
                  pl.BlockSpec((k, tn), lambda i, j: (0, j))],
        out_specs=[pl.BlockSpec((tm, tn), lambda i, j: (i, j)),
                   pl.BlockSpec((tm, f32_width), lambda i, j: (i, 0))],
        out_shape=[jax.ShapeDtypeStruct((n, m), BF16), jax.ShapeDtypeStruct((n, f32_width), F32)],
        compiler_params=_params("parallel", "arbitrary"),
        name="proj_in_mixed",
    )(xb, wb)


def _mm_nt_kernel(w_ref, x_ref, *rest):
    k_ref, v_ref = rest[-2:]
    kv = _dot_nt(w_ref[...], x_ref[...])
    half = k_ref.shape[0]
    k_ref[...] = kv[:half]
    v_ref[...] = kv[half:]


def _proj_kv_t(wt, xb, batch, seq, layer, depth, prev=None):
    m, k = wt.shape
    w = m // 2
    tm = min(seq, 1024)
    nt = seq // tm
    out_spec = pl.BlockSpec((None, None, w, tm), lambda b, t: (layer, b, 0, t))
    out_shape = jax.ShapeDtypeStruct((depth, batch, w, seq), F32)
    in_specs = [pl.BlockSpec((m, k), lambda b, t: (0, 0)),
                pl.BlockSpec((tm, k), lambda b, t: (b * nt + t, 0))]
    args, aliases = [wt, xb], {}
    if prev is not None:
        in_specs += [pl.BlockSpec(memory_space=pl.ANY)] * 2
        args += list(prev)
        aliases = {2: 0, 3: 1}
    return pl.pallas_call(
        _mm_nt_kernel,
        grid=(batch, nt),
        in_specs=in_specs,
        out_specs=[out_spec, out_spec],
        out_shape=[out_shape, out_shape],
        input_output_aliases=aliases,
        compiler_params=_params("parallel", "arbitrary"),
        name="proj_kv_t",
    )(*args)


def _softplus2(z):
    neg_abs = lax.bitcast_convert_type(
        lax.bitcast_convert_type(z, jnp.uint32) | jnp.uint32(0x80000000), F32)
    return jnp.maximum(z, 0.0) + jnp.log2(1.0 + jnp.exp2(neg_abs))


def _suffix_sums(sp, tri2):
    n = sp.shape[1]
    hi, lo = _split2(sp)
    c = _dot(jnp.concatenate([hi, lo], axis=1), tri2)
    return c[:, :n], c[:, n:]


def _tri_suffix2(n):
    j = np.arange(n)[:, None]
    s = np.arange(n)[None, :]
    ext = np.concatenate([(j >= s).astype(np.float32), np.ones((n, LANES), np.float32)], axis=1)
    return jnp.asarray(np.concatenate([ext, ext], axis=0), dtype=BF16)


def _sb_prompt_kernel(bias_ref, q_ref, kt_ref, vt_ref, tri_ref, o_ref, kb_ref, vb_ref, carry_ref,
                      acc_ref, *, tq, c2):
    hp = pl.program_id(1)
    qi = pl.program_id(2)
    n_blk = kb_ref.shape[0]
    sub = tri_ref.shape[0] // 2

    @pl.when(qi == 0)
    def _():
        for j in range(n_blk):
            kb_ref[j] = kt_ref[:, j * tq:(j + 1) * tq].astype(BF16)
            vb_ref[j] = vt_ref[:, j * tq:(j + 1) * tq].astype(BF16)

    lane = lax.broadcasted_iota(jnp.int32, (1, LANES), 1)
    half = LANES // 2
    q = q_ref[...].astype(F32) * c2
    qh = [jnp.where((lane >= h * half) & (lane < (h + 1) * half), q, 0.0).astype(BF16)
          for h in range(2)]
    bias2 = [bias_ref[2 * hp + h] * LOG2E for h in range(2)]
    tri2 = tri_ref[...]
    carry_ref[...] = jnp.zeros_like(carry_ref)
    acc_ref[...] = jnp.zeros_like(acc_ref)

    def span(j, groups, staged):
        kb = kb_ref[j]
        vb = vb_ref[j]

        def mask_of(r0, r1, nkeys):
            return (lax.broadcasted_iota(jnp.int32, (r1 - r0, nkeys), 1)
                    < lax.broadcasted_iota(jnp.int32, (r1 - r0, nkeys), 0) + r0)

        def logits(h, r0, r1, nkeys, masked):
            z = _dot(qh[h][r0:r1], kb[:, :nkeys]) + bias2[h]
            sp = _softplus2(z)
            if masked:
                sp = jnp.where(mask_of(r0, r1, nkeys), sp, 0.0)
            return z, [_suffix_sums(sp[:, i * sub:(i + 1) * sub], tri2) for i in range(nkeys // sub)]

        def weigh(h, r0, r1, nkeys, masked, z, sums):
            carry = carry_ref[h, r0:r1]
            parts = []
            for i in reversed(range(nkeys // sub)):
                c, tot = sums[i]
                parts.append(z[:, i * sub:(i + 1) * sub] - c - carry)
                carry = carry + tot
            carry_ref[h, r0:r1] = carry
            w = jnp.exp2(jnp.concatenate(parts[::-1], axis=1))
            if masked:
                w = jnp.where(mask_of(r0, r1, nkeys), w, 0.0)
            acc_ref[h, r0:r1] += _dot_nt(w.astype(BF16), vb[:, :nkeys])

        work = [(h,) + g for g in groups for h in range(2)]
        if staged:
            done = [logits(*wk) for wk in work]
            for wk, d in zip(work, done):
                weigh(*wk, *d)
        else:
            for wk in work:
                weigh(*wk, *logits(*wk))

    n_groups = SB_DIAG_GROUPS
    rows = tq // n_groups
    span(qi, [(g * rows, (g + 1) * rows, (g + 1) * rows, True) for g in range(n_groups)], True)

    def body(i, _):
        span(qi - 1 - i, [(0, tq, tq, False)], False)
        return 0

    lax.fori_loop(0, qi, body, 0)
    o_ref[...] = jnp.where(lane < half, acc_ref[0], acc_ref[1]).astype(o_ref.dtype)


def _sb_prompt(q_src, kt, vt, layer, sb_bias, batch, seq, n_heads, d_head):
    n = q_src.shape[0]
    w_a = n_heads * d_head
    tq = SB_TQ
    nq = seq // tq
    npair = w_a // LANES
    tri2 = _tri_suffix2(LANES)
    kern = functools.partial(_sb_prompt_kernel, tq=tq, c2=LOG2E * d_head ** -0.5)
    return pl.pallas_call(
        kern,
        grid_spec=pltpu.PrefetchScalarGridSpec(
            num_scalar_prefetch=1,
            grid=(batch, npair, nq),
            in_specs=[
                pl.BlockSpec((tq, LANES), lambda b, hp, qi, bias: (b * nq + qi, hp)),
                pl.BlockSpec((None, None, LANES, seq), lambda b, hp, qi, bias: (layer, b, hp, 0)),
                pl.BlockSpec((None, None, LANES, seq), lambda b, hp, qi, bias: (layer, b, hp, 0)),
                pl.BlockSpec(tri2.shape, lambda b, hp, qi, bias: (0, 0)),
            ],
            out_specs=pl.BlockSpec((tq, LANES), lambda b, hp, qi, bias: (b * nq + qi, hp)),
            scratch_shapes=[pltpu.VMEM((nq, LANES, tq), BF16), pltpu.VMEM((nq, LANES, tq), BF16),
                            pltpu.VMEM((2, tq, LANES), F32), pltpu.VMEM((2, tq, LANES), F32)],
        ),
        out_shape=jax.ShapeDtypeStruct((n, w_a), BF16),
        compiler_params=_params("parallel", "parallel", "arbitrary"),
        name="sb_prompt",
    )(sb_bias, q_src, kt, vt, tri2)


def _sb_sample_kernel(pt_ref, q_ref, kn_ref, vn_ref, bias_ref, tri_ref, *rest,
                      n_pages, n_heads, d_head, t_new, c2):
    kpages = rest[:n_pages]
    vpages = rest[n_pages:2 * n_pages]
    o_ref = rest[2 * n_pages]
    rows = n_heads * t_new
    w_a = n_heads * d_head
    page = tri_ref.shape[0] // 2
    head_mask = (lax.broadcasted_iota(jnp.int32, (rows, w_a), 0) // t_new
                 == lax.broadcasted_iota(jnp.int32, (rows, w_a), 1) // d_head)
    bias2 = bias_ref[...] * LOG2E
    tri2 = tri_ref[...]

    q = q_ref[...] * c2
    qbd = jnp.where(head_mask, jnp.concatenate([q] * n_heads, axis=0), 0.0).astype(BF16)

    pad = jnp.zeros((page - t_new, w_a), F32)
    kn = jnp.concatenate([kn_ref[...], pad], axis=0).astype(BF16)
    vn = jnp.concatenate([vn_ref[...], pad], axis=0).astype(BF16)
    mask = (lax.broadcasted_iota(jnp.int32, (rows, page), 1)
            < lax.broadcasted_iota(jnp.int32, (rows, page), 0) % t_new)
    zs = [_dot_nt(qbd, kn) + bias2]
    zs += [_dot(qbd, kpages[r][...].astype(BF16)) + bias2 for r in range(n_pages)]
    sps = [jnp.where(mask, _softplus2(zs[0]), 0.0)] + [_softplus2(z) for z in zs[1:]]
    c_all = _dot(jnp.concatenate([jnp.concatenate(_split2(sp), axis=1) for sp in sps], axis=0), tri2)
    carry = jnp.zeros((rows, LANES), F32)
    acc = jnp.zeros((rows, w_a), F32)
    for r in range(n_pages + 1):
        c = c_all[r * rows:(r + 1) * rows]
        w = jnp.exp2(zs[r] - c[:, :page] - carry)
        carry = carry + c[:, page:]
        if r == 0:
            acc = acc + _dot(jnp.where(mask, w, 0.0).astype(BF16), vn)
        else:
            acc = acc + _dot_nt(w.astype(BF16), vpages[r - 1][...].astype(BF16))
    a = jnp.where(head_mask, acc, 0.0)
    out = a[0:t_new]
    for h in range(1, n_heads):
        out = out + a[h * t_new:(h + 1) * t_new]
    o_ref[...] = out


def _sb_sample(ys, ckt, cvt, page_table, sb_bias, layer, n_heads, d_head, t_new):
    n = ys.shape[0]
    n_seq, n_pages = page_table.shape
    page = ckt.shape[3]
    w_a = n_heads * d_head
    rows = n_heads * t_new
    bias_rep = jnp.broadcast_to(jnp.repeat(sb_bias.astype(F32), t_new)[:, None], (rows, LANES))
    tri2 = _tri_suffix2(page)

    def page_spec(r):
        return pl.BlockSpec((None, None, w_a, page),
                            lambda b, pt, r=r: (layer, pt[b, n_pages - 1 - r], 0, 0))

    kern = functools.partial(_sb_sample_kernel, n_pages=n_pages, n_heads=n_heads, d_head=d_head,
                             t_new=t_new, c2=LOG2E * d_head ** -0.5)
    return pl.pallas_call(
        kern,
        grid_spec=pltpu.PrefetchScalarGridSpec(
            num_scalar_prefetch=1,
            grid=(n_seq,),
            in_specs=[
                pl.BlockSpec((t_new, w_a), lambda b, pt: (b, 0)),
                pl.BlockSpec((t_new, w_a), lambda b, pt: (b, 1)),
                pl.BlockSpec((t_new, w_a), lambda b, pt: (b, 2)),
                pl.BlockSpec((rows, LANES), lambda b, pt: (0, 0)),
                pl.BlockSpec(tri2.shape, lambda b, pt: (0, 0)),
            ] + [page_spec(r) for r in range(n_pages)] * 2,
            out_specs=pl.BlockSpec((t_new, w_a), lambda b, pt: (b, 0)),
        ),
        out_shape=jax.ShapeDtypeStruct((n, w_a), F32),
        compiler_params=_params("arbitrary"),
        name="sb_sample",
    )(page_table, ys, ys, ys, bias_rep, tri2, *([ckt] * n_pages), *([cvt] * n_pages))


def _hgrn_masks(c, seg):
    t = np.arange(c)[:, None]
    j = np.arange(c)[None, :]
    same = (t // seg) == (j // seg)
    halves = []
    h = seg // 2
    while h >= 1:
        halves.append(h)
        h //= 2
    blocks = [same & (j <= t), same & (j > t)]
    ublocks, lblocks, pairs = [], [], []
    for h in halves:
        upper_t = ((t // h) % 2) == 1
        m_t = (t // h) * h
        ublocks.append(upper_t & (j >= m_t) & (j <= t))
        lower_t = ~upper_t
        m_next = (t // h + 1) * h
        lblocks.append(lower_t & (j > t) & (j < m_next))
        s = j
        pairs.append(upper_t & ((((s // h) % 2) == 0)) & ((t // (2 * h)) == (s // (2 * h))))
    pairs.append(t == j)
    m_all = np.concatenate(blocks + ublocks + lblocks, axis=0).astype(np.float32)
    m_cat = np.concatenate([m_all, m_all], axis=1)
    return (jnp.asarray(m_cat, dtype=BF16), jnp.asarray(np.stack(pairs).astype(np.float32)),
            len(halves))


def _hgrn_gates(zf, lb):
    log_sig = jnp.minimum(zf, 0.0) - jnp.log(1.0 + jnp.exp(-jnp.abs(zf)))
    a1 = jnp.log(lb)
    a2 = jnp.log(1.0 - lb) + log_sig
    g = jnp.maximum(a1, a2) + jnp.log(1.0 + jnp.exp(-jnp.abs(a1 - a2)))
    return g, (1.0 - lb) * jax.nn.sigmoid(-zf)


def _hgrn_exponents(gs, mcat):
    cols = [jnp.concatenate(_split2(g), axis=0) for g in gs]
    return _dot(mcat, cols[0] if len(cols) == 1 else jnp.concatenate(cols, axis=1))


def _hgrn_intra(q, k, v, blk, pairs, n_levels):
    vb = v.astype(BF16)
    scores = _dot_nt(q.astype(BF16), k.astype(BF16)) * pairs[n_levels]
    for i in range(n_levels):
        qe = (q * jnp.exp(blk(2 + i))).astype(BF16)
        ke = (k * jnp.exp(blk(2 + n_levels + i))).astype(BF16)
        scores = scores + _dot_nt(qe, ke) * pairs[i]
    eb = jnp.exp(blk(0))
    kd = k * jnp.exp(blk(1))
    return _dot(scores.astype(BF16), vb), (q * eb).astype(BF16), eb.T, kd.T.astype(BF16), vb


def _hgrn_finish(o, og, norm_w):
    o = o * lax.rsqrt(jnp.mean(o * o, axis=-1, keepdims=True) + RMS_EPS) * norm_w
    return o * (og * jax.nn.sigmoid(og))


def _hgrn_prompt_kernel(q_ref, f_ref, v_ref, og_ref, lb_ref, nw_ref, mcat_ref, pairs_ref,
                        y_ref, s_out_ref, s_ref, *, n_levels, chunk):
    ti = pl.program_id(2)

    @pl.when(ti == 0)
    def _():
        s_ref[...] = jnp.zeros_like(s_ref)

    pairs = pairs_ref[...]
    n_chunks = q_ref.shape[0] // chunk
    rows = lambda a, r: a[r * chunk:(r + 1) * chunk]
    g, k = _hgrn_gates(f_ref[...], lb_ref[...])
    e = _hgrn_exponents([rows(g, r) for r in range(n_chunks)], mcat_ref[...])
    q = q_ref[...].astype(F32)
    v = v_ref[...].astype(F32)
    intra = [_hgrn_intra(rows(q, r), rows(k, r), rows(v, r),
                         lambda i, r=r: e[i * chunk:(i + 1) * chunk, r * LANES:(r + 1) * LANES],
                         pairs, n_levels) for r in range(n_chunks)]
    s = s_ref[...]
    for r, (o, qe, eb_t, kd_t, vb) in enumerate(intra):
        o = o + _dot(qe, s.astype(BF16))
        s = s * eb_t[:, chunk - 1:chunk] + _dot(kd_t, vb)
        sl = pl.ds(r * chunk, chunk)
        y_ref[sl, :] = _hgrn_finish(o, og_ref[sl, :].astype(F32), nw_ref[...]).astype(y_ref.dtype)
    s_ref[...] = s

    @pl.when(ti == pl.num_programs(2) - 1)
    def _():
        s_out_ref[...] = s


def _hgrn_sample_kernel(q_ref, f_ref, v_ref, og_ref, lb_ref, nw_ref, mcat_ref, pairs_ref,
                        s0_ref, *rest, n_levels, seg):
    y_ref, s_out_ref = rest[-2:]
    chunk = q_ref.shape[0]
    n_seq = chunk // seg
    lane_seq = lax.broadcasted_iota(jnp.int32, (1, chunk), 1) // seg
    row_seq = lax.broadcasted_iota(jnp.int32, (chunk, 1), 0) // seg
    g, k = _hgrn_gates(f_ref[...], lb_ref[...])
    e = _hgrn_exponents([g], mcat_ref[...])
    o, qe, eb_t, kd_t, vb = _hgrn_intra(q_ref[...], k, v_ref[...],
                                        lambda i: e[i * chunk:(i + 1) * chunk],
                                        pairs_ref[...], n_levels)
    for i in range(n_seq):
        s0 = s0_ref[i]
        o = o + _dot(jnp.where(row_seq == i, qe, jnp.zeros_like(qe)), s0.astype(BF16))
        last = (i + 1) * seg - 1
        kd_i = jnp.where(lane_seq == i, kd_t, jnp.zeros_like(kd_t))
        s_out_ref[i] = s0 * eb_t[:, last:last + 1] + _dot(kd_i, vb)
    y_ref[...] = _hgrn_finish(o, og_ref[...], nw_ref[...])


def _col_starts(names, sizes):
    starts, off = {}, 0
    for name in names:
        starts[name] = off
        off += sizes[name]
    return starts


def _hgrn_col_blocks(cols):
    return tuple(cols[k] // LANES for k in ("q_c", "f_c", "i_c", "og_c"))


def _hgrn_prompt(y, yf, lower, norm_w, batch, seq, n_heads_c, cols):
    n = y.shape[0]
    chunk = HG_CHUNK
    tt = 512
    nt = seq // tt
    mcat, pairs, n_levels = _hgrn_masks(chunk, chunk)
    cq, _, ci, cg = _hgrn_col_blocks(cols)

    def col(cb):
        return pl.BlockSpec((tt, LANES), lambda b, h, t, cb=cb: (b * nt + t, cb + h))

    const2 = lambda shape: pl.BlockSpec(shape, lambda b, h, t: (0, 0))
    kern = functools.partial(_hgrn_prompt_kernel, n_levels=n_levels, chunk=chunk)
    return pl.pallas_call(
        kern,
        grid=(batch, n_heads_c, nt),
        in_specs=[col(cq), col(0), col(ci), col(cg),
                  pl.BlockSpec((1, LANES), lambda b, h, t: (0, h)),
                  const2((1, LANES)), const2(mcat.shape),
                  pl.BlockSpec(pairs.shape, lambda b, h, t: (0, 0, 0))],
        out_specs=[pl.BlockSpec((tt, LANES), lambda b, h, t: (b * nt + t, h)),
                   pl.BlockSpec((None, None, LANES, LANES), lambda b, h, t: (b, h, 0, 0))],
        out_shape=[jax.ShapeDtypeStruct((n, n_heads_c * LANES), BF16),
                   jax.ShapeDtypeStruct((batch, n_heads_c, LANES, LANES), F32)],
        scratch_shapes=[pltpu.VMEM((LANES, LANES), F32)],
        compiler_params=_params("parallel", "parallel", "arbitrary"),
        name="hgrn_prompt",
    )(y, yf, y, y, lower, norm_w, mcat, pairs)


def _hgrn_sample(ys, lower, norm_w, state, layer, t_new, n_heads_c, cols, prev=None):
    n = ys.shape[0]
    chunk = HG_CHUNK
    g = chunk // t_new
    mcat, pairs, n_levels = _hgrn_masks(chunk, t_new)
    cq, cf, ci, cg = _hgrn_col_blocks(cols)

    def col(cb):
        return pl.BlockSpec((chunk, LANES), lambda i, h, cb=cb: (i, cb + h))

    const2 = lambda shape: pl.BlockSpec(shape, lambda i, h: (0, 0))
    state_spec = pl.BlockSpec((None, g, None, LANES, LANES), lambda i, h: (layer, i, h, 0, 0))
    in_specs = [col(cq), col(cf), col(ci), col(cg),
                pl.BlockSpec((1, LANES), lambda i, h: (0, h)),
                const2((1, LANES)), const2(mcat.shape),
                pl.BlockSpec(pairs.shape, lambda i, h: (0, 0, 0)), state_spec]
    args, aliases = [ys, ys, ys, ys, lower, norm_w, mcat, pairs, state], {}
    if prev is not None:
        in_specs.append(pl.BlockSpec(memory_space=pl.ANY))
        args.append(prev)
        aliases = {len(args) - 1: 1}
    kern = functools.partial(_hgrn_sample_kernel, n_levels=n_levels, seg=t_new)
    return pl.pallas_call(
        kern,
        grid=(n // chunk, n_heads_c),
        in_specs=in_specs,
        out_specs=[pl.BlockSpec((chunk, LANES), lambda i, h: (i, h)), state_spec],
        out_shape=[jax.ShapeDtypeStruct((n, n_heads_c * LANES), F32),
                   jax.ShapeDtypeStruct(state.shape, F32)],
        input_output_aliases=aliases,
        compiler_params=_params("parallel", "parallel"),
        name="hgrn_sample",
    )(*args)


def _conv_prompt(u, w, carry_ref):
    tm = u.shape[0]
    r = lax.broadcasted_iota(jnp.int32, (tm, 1), 0)
    c6 = carry_ref[SUBLANES - 2:SUBLANES - 1, :]
    c7 = carry_ref[SUBLANES - 1:SUBLANES, :]
    u1 = jnp.where(r == 0, c7, pltpu.roll(u, 1, 0))
    u2 = jnp.where(r == 0, c6, jnp.where(r == 1, c7, pltpu.roll(u, 2, 0)))
    carry_ref[...] = u[tm - SUBLANES:, :]
    return w[2:3, :] * u + w[1:2, :] * u1 + w[0:1, :] * u2


def _conv_sample(u, w, hist, t_new):
    tm = u.shape[0]
    pos = lax.broadcasted_iota(jnp.int32, (tm, 1), 0) % t_new
    u1 = jnp.where(pos == 0, pltpu.roll(hist, tm - 1, 0), pltpu.roll(u, 1, 0))
    u2 = jnp.where(pos < 2, hist, pltpu.roll(u, 2, 0))
    return w[2:3, :] * u + w[1:2, :] * u1 + w[0:1, :] * u2


def _layer_norm(h, g, b):
    hc = h - jnp.mean(h, axis=-1, keepdims=True)
    var = jnp.mean(hc * hc, axis=-1, keepdims=True)
    return hc * lax.rsqrt(var + LN_EPS) * g + b


def _merge_kernel(x_ref, gb_ref, gc_ref, hb_ref, ma_ref, mb_ref, mc_ref, ya_ref, yc_ref,
                  cw_ref, wa_ref, wb_ref, wc_ref, wo_ref, g_ref, b_ref, *rest,
                  sample, t_new, tiles_per_seq, alpha):
    if sample:
        hist_ref, x1_ref, x1b_ref, u_ref = rest
    else:
        x1_ref, x1b_ref, tail_ref, carry_ref = rest

        @pl.when(pl.program_id(0) % tiles_per_seq == 0)
        def _():
            carry_ref[...] = jnp.zeros_like(carry_ref)

    u = gc_ref[...].astype(F32) * hb_ref[...].astype(F32)
    cw = cw_ref[...]
    if sample:
        conv = _conv_sample(u, cw, hist_ref[...], t_new)
        u_ref[...] = u
    else:
        conv = _conv_prompt(u, cw, carry_ref)
        tail_ref[...] = u[u.shape[0] - SUBLANES:, :]
    yb = gb_ref[...].astype(F32) * conv
    gate = lambda ref: jax.nn.sigmoid(ref[...].astype(F32))
    mix = (gate(ma_ref) * _dot(ya_ref[...].astype(BF16), wa_ref[...])
           + gate(mb_ref) * _dot(yb.astype(BF16), wb_ref[...])
           + gate(mc_ref) * _dot(yc_ref[...].astype(BF16), wc_ref[...]))
    h = alpha * x_ref[...] + _dot(mix.astype(BF16), wo_ref[...])
    x1 = _layer_norm(h, g_ref[...], b_ref[...])
    x1_ref[...] = x1
    x1b_ref[...] = x1.astype(BF16)


def _merge(x, y, ya, yc, conv_w, wa, wb, wc, wo, ln_g, ln_b, alpha, cols, seq, hist=None,
           t_new=None):
    n, d = x.shape
    w_b = conv_w.shape[1]
    sample = hist is not None
    tm = min(n, MERGE_TM)
    nt = n // tm
    off_b = cols["gate_b"] // w_b
    off_m = cols["m_a"] // d
    row = lambda width, cb: pl.BlockSpec((tm, width), lambda i, cb=cb: (i, cb))
    const = lambda a: pl.BlockSpec(a.shape, lambda i: (0, 0))
    in_specs = [row(d, 0), row(w_b, off_b), row(w_b, off_b + 1), row(w_b, off_b + 2),
                row(d, off_m), row(d, off_m + 1), row(d, off_m + 2),
                row(ya.shape[1], 0), row(yc.shape[1], 0),
                const(conv_w), const(wa), const(wb), const(wc), const(wo), const(ln_g), const(ln_b)]
    args = [x, y, y, y, y, y, y, ya, yc, conv_w, wa, wb, wc, wo, ln_g, ln_b]
    out_specs = [row(d, 0), row(d, 0)]
    out_shape = [jax.ShapeDtypeStruct((n, d), F32), jax.ShapeDtypeStruct((n, d), BF16)]
    scratch = []
    if sample:
        in_specs.append(row(w_b, 0))
        args.append(hist)
        out_specs.append(row(w_b, 0))
        out_shape.append(jax.ShapeDtypeStruct((n, w_b), F32))
    else:
        out_specs.append(pl.BlockSpec((SUBLANES, w_b), lambda i: (i, 0)))
        out_shape.append(jax.ShapeDtypeStruct((nt * SUBLANES, w_b), F32))
        scratch.append(pltpu.VMEM((SUBLANES, w_b), F32))
    kern = functools.partial(_merge_kernel, sample=sample, t_new=t_new,
                             tiles_per_seq=max(seq // tm, 1), alpha=alpha)
    return pl.pallas_call(
        kern, grid=(nt,), in_specs=in_specs, out_specs=out_specs, out_shape=out_shape,
        scratch_shapes=scratch, compiler_params=_params("arbitrary"),
        name="merge_sample" if sample else "merge_prompt",
    )(*args)


def _ffn_kernel(xb_ref, x_ref, wu_ref, cw_ref, cb_ref, wd_ref, g_ref, b_ref, *rest,
                sample, t_new, tiles_per_seq, alpha):
    if sample:
        hist_ref, o_ref, ob_ref, a_ref = rest
    else:
        o_ref, ob_ref, tail_ref, carry_ref = rest
    d_ff = wd_ref.shape[0]
    if not sample:
        @pl.when(pl.program_id(0) % tiles_per_seq == 0)
        def _():
            carry_ref[...] = jnp.zeros_like(carry_ref)

    xb = xb_ref[...]
    y = jnp.zeros(x_ref.shape, F32)
    chunk = min(d_ff, FFN_CHUNK)
    for c in range(d_ff // chunk):
        lo, hi = c * chunk, (c + 1) * chunk
        a = _dot(xb, wu_ref[:, lo:hi])
        gate = _dot(xb, wu_ref[:, d_ff + lo:d_ff + hi])
        cw = cw_ref[:, lo:hi]
        if sample:
            conv = _conv_sample(a, cw, hist_ref[:, lo:hi], t_new)
            a_ref[:, lo:hi] = a
        else:
            conv = _conv_prompt(a, cw, carry_ref.at[:, lo:hi])
            tail_ref[:, lo:hi] = a[a.shape[0] - SUBLANES:, :]
        h = jax.nn.gelu(conv + cb_ref[:, lo:hi]) * gate
        y = y + _dot(h.astype(BF16), wd_ref[lo:hi, :])
    x2 = _layer_norm(alpha * x_ref[...] + y, g_ref[...], b_ref[...])
    o_ref[...] = x2
    ob_ref[...] = x2.astype(BF16)


def _ffn(xb, x, w_up, conv_w, conv_b, w_down, ln_g, ln_b, alpha, seq, hist=None, t_new=None):
    n, d = x.shape
    d_ff = w_down.shape[0]
    sample = hist is not None
    tm = min(n, FFN_TM // 2 if sample else FFN_TM)
    nt = n // tm
    row = lambda width: pl.BlockSpec((tm, width), lambda i: (i, 0))
    const = lambda a: pl.BlockSpec(a.shape, lambda i: (0, 0), pipeline_mode=pl.Buffered(1))
    in_specs = [row(d), row(d), const(w_up), const(conv_w), const(conv_b), const(w_down),
                const(ln_g), const(ln_b)]
    args = [xb, x, w_up, conv_w, conv_b, w_down, ln_g, ln_b]
    out_specs = [row(d), row(d)]
    out_shape = [jax.ShapeDtypeStruct((n, d), F32), jax.ShapeDtypeStruct((n, d), BF16)]
    scratch = []
    if sample:
        in_specs.append(row(d_ff))
        args.append(hist)
        out_specs.append(row(d_ff))
        out_shape.append(jax.ShapeDtypeStruct((n, d_ff), F32))
    else:
        out_specs.append(pl.BlockSpec((SUBLANES, d_ff), lambda i: (i, 0)))
        out_shape.append(jax.ShapeDtypeStruct((nt * SUBLANES, d_ff), F32))
        scratch.append(pltpu.VMEM((SUBLANES, d_ff), F32))
    kern = functools.partial(_ffn_kernel, sample=sample, t_new=t_new,
                             tiles_per_seq=max(seq // tm, 1), alpha=alpha)
    return pl.pallas_call(
        kern, grid=(nt,), in_specs=in_specs, out_specs=out_specs, out_shape=out_shape,
        scratch_shapes=scratch, compiler_params=_params("arbitrary"),
        name="ffn_sample" if sample else "ffn_prompt",
    )(*args)


def _pad_hist(state, t_new):
    n_seq, nb, c = state.shape
    return jnp.pad(state, ((0, 0), (0, t_new - nb), (0, 0))).reshape(n_seq * t_new, c)


def _tails(tail, batch, tiles_per_seq):
    c = tail.shape[1]
    t = tail.reshape(batch, tiles_per_seq, SUBLANES, c)
    return t[:, tiles_per_seq - 1, SUBLANES - 2:, :]


def kernel(x_prompt, x_sample, cache_k, cache_v, state_conv, state_hgrn, state_ffn_conv, page_table,
           w_in, sb_bias, conv_w, hgrn_lb, hgrn_norm_w, w_br_a, w_br_b, w_br_c, w_o, ln1_g, ln1_b,
           w_up, ffn_conv_w, ffn_conv_b, w_down, ln2_g, ln2_b):
    depth, d_model, d_in = w_in.shape
    batch, seq, _ = x_prompt.shape
    n_seq, t_new, _ = x_sample.shape
    n_pool, page, n_heads, d_head = cache_k.shape[1:]
    w_a = n_heads * d_head
    w_b = conv_w.shape[2]
    n_heads_c, dk_c, dv_c = state_hgrn.shape[2:]
    d_ff = w_down.shape[1]
    names = ("q_a", "k_a", "v_a", "gate_b", "gate_c", "h_b", "q_c", "f_c", "i_c", "og_c",
             "m_a", "m_b", "m_c")
    widths = (w_a, w_a, w_a, w_b, w_b, w_b, n_heads_c * dk_c, n_heads_c * dk_c, n_heads_c * dv_c,
              n_heads_c * dv_c, d_model, d_model, d_model)
    sizes = dict(zip(names, widths))
    assert sum(widths) == d_in and dk_c == LANES and dv_c == LANES and 2 * d_head == LANES
    cols_s = _col_starts(names, sizes)
    cols_p = _col_starts([k for k in names if k not in ("k_a", "v_a")], sizes)
    alpha = (2 * depth) ** 0.25

    lbs = jnp.cumsum(jax.nn.softmax(hgrn_lb.astype(F32), axis=0), axis=0)
    lower = lbs - lbs[0]

    ckt = jnp.transpose(cache_k, (0, 1, 3, 4, 2)).reshape(depth, n_pool, w_a, page)
    cvt = jnp.transpose(cache_v, (0, 1, 3, 4, 2)).reshape(depth, n_pool, w_a, page)
    bf = lambda a: a.astype(BF16)
    w_in_b, wa_b, wb_b, wc_b, wo_b, wu_b, wd_b = map(bf, (w_in, w_br_a, w_br_b, w_br_c, w_o, w_up, w_down))
    kv0, kv1 = cols_s["k_a"], cols_s["gate_b"]
    w_rest_b = jnp.concatenate([w_in_b[:, :, :kv0], w_in_b[:, :, kv1:]], axis=2)
    w_kvt_b = jnp.transpose(w_in_b[:, :, kv0:kv1], (0, 2, 1))

    hp = x_prompt.reshape(batch * seq, d_model)
    hs = x_sample.reshape(n_seq * t_new, d_model)
    hpb, hsb = bf(hp), bf(hs)
    outs = {k: [] for k in ("conv_p", "s_p", "ffn_p", "k_s", "v_s", "conv_s", "ffn_s")}
    kv_all = None
    s_all = None
    for l in range(depth):
        row = lambda a: a[l].reshape(1, -1)
        yp, yf = _proj_in_mixed(hpb, w_rest_b[l], cols_p["f_c"], sizes["f_c"])
        kv_all = _proj_kv_t(w_kvt_b[l], hpb, batch, seq, l, depth, kv_all)
        ys = _proj_in(hsb, w_in_b[l])
        ya_p = _sb_prompt(yp, *kv_all, l, sb_bias[l], batch, seq, n_heads, d_head)
        ya_s = _sb_sample(ys, ckt, cvt, page_table, sb_bias[l], l, n_heads, d_head, t_new)
        yc_p, s_p = _hgrn_prompt(yp, yf, row(lower), row(hgrn_norm_w), batch, seq, n_heads_c, cols_p)
        yc_s, s_all = _hgrn_sample(ys, row(lower), row(hgrn_norm_w), state_hgrn, l, t_new, n_heads_c,
                                   cols_s, s_all)
        margs = (conv_w[l], wa_b[l], wb_b[l], wc_b[l], wo_b[l], row(ln1_g), row(ln1_b), alpha)
        x1p, x1pb, tail_b = _merge(hp, yp, ya_p, yc_p, *margs, cols_p, seq)
        x1s, x1sb, u_s = _merge(hs, ys, ya_s, yc_s, *margs, cols_s, t_new,
                                hist=_pad_hist(state_conv[l], t_new), t_new=t_new)
        fargs = (wu_b[l], ffn_conv_w[l], row(ffn_conv_b), wd_b[l], row(ln2_g), row(ln2_b), alpha)
        hp, hpb, tail_f = _ffn(x1pb, x1p, *fargs, seq)
        hs, hsb, a_s = _ffn(x1sb, x1s, *fargs, t_new,
                            hist=_pad_hist(state_ffn_conv[l], t_new), t_new=t_new)

        outs["conv_p"].append(_tails(tail_b, batch, seq // min(batch * seq, MERGE_TM)))
        outs["s_p"].append(s_p)
        outs["ffn_p"].append(_tails(tail_f, batch, seq // min(batch * seq, FFN_TM)))
        outs["k_s"].append(ys[:, kv0:kv0 + w_a].reshape(n_seq, t_new, n_heads, d_head))
        outs["v_s"].append(ys[:, kv0 + w_a:kv1].reshape(n_seq, t_new, n_heads, d_head))
        outs["conv_s"].append(u_s.reshape(n_seq, t_new, w_b)[:, t_new - 2:])
        outs["ffn_s"].append(a_s.reshape(n_seq, t_new, d_ff)[:, t_new - 2:])

    def heads_t(a):
        return jnp.transpose(a.reshape(depth, batch, n_heads, d_head, seq), (0, 1, 4, 2, 3))

    st = {k: jnp.stack(v) for k, v in outs.items()}
    return (hp.reshape(batch, seq, d_model), hs.reshape(n_seq, t_new, d_model),
            heads_t(kv_all[0]), heads_t(kv_all[1]), st["conv_p"], st["s_p"], st["ffn_p"],
            st["k_s"], st["v_s"], st["conv_s"], s_all, st["ffn_s"])
```

```python
import functools
import math

import numpy as np
import jax
import jax.numpy as jnp
from jax import lax
from jax.experimental import pallas as pl
from jax.experimental.pallas import tpu as pltpu

F32 = jnp.float32
BF16 = jnp.bfloat16

LANES = 128
SUBLANES = 8
CONV_TAPS = 3
LN_EPS = 1e-5
RMS_EPS = 1e-6
LOG2E = math.log2(math.e)
HG_CHUNK = 128
HG_TILE = 1024
MERGE_TM = 256
FFN_TM = 512
SB_TQ = 512
FFN_CHUNK = 2816
VMEM_LIMIT = 56 * 1024 * 1024


def _params(*sem):
    return pltpu.CompilerParams(dimension_semantics=sem, vmem_limit_bytes=VMEM_LIMIT)


def _split2(x):
    hi = lax.bitcast_convert_type(
        lax.bitcast_convert_type(x, jnp.uint32) & jnp.uint32(0xFFFF0000), F32)
    return hi.astype(BF16), (x - hi).astype(BF16)


def _dot(a, b):
    return jnp.dot(a, b, preferred_element_type=F32)


def _dot_nt(a, b):
    return lax.dot_general(a, b, (((1,), (1,)), ((), ())), preferred_element_type=F32)


def _mm_kernel(x_ref, w_ref, o_ref):
    o_ref[...] = _dot(x_ref[...], w_ref[...])


def _proj_in(xb, wb):
    n, k = xb.shape
    m = wb.shape[1]
    tm = min(n, 1024)
    tn = 1024
    return pl.pallas_call(
        _mm_kernel,
        grid=(n // tm, m // tn),
        in_specs=[pl.BlockSpec((tm, k), lambda i, j: (i, 0)),
                  pl.BlockSpec((k, tn), lambda i, j: (0, j))],
        out_specs=pl.BlockSpec((tm, tn), lambda i, j: (i, j)),
        out_shape=jax.ShapeDtypeStruct((n, m), F32),
        compiler_params=_params("parallel", "arbitrary"),
        name="proj_in",
    )(xb, wb)


def _mm_mixed_kernel(x_ref, w_ref, o_ref, f_ref, *, jf, off):
    acc = _dot(x_ref[...], w_ref[...])
    o_ref[...] = acc.astype(BF16)

    @pl.when(pl.program_id(1) == jf)
    def _():
        f_ref[...] = acc[:, off:off + f_ref.shape[1]]


def _proj_in_mixed(xb, wb, f32_start, f32_width):
    n, k = xb.shape
    m = wb.shape[1]
    tm = min(n, 1024)
    tn = 1024
    jf, off = divmod(f32_start, tn)
    assert off + f32_width <= tn
    return pl.pallas_call(
        functools.partial(_mm_mixed_kernel, jf=jf, off=off),
        grid=(n // tm, m // tn),
        in_specs=[pl.BlockSpec((tm, k), lambda i, j: (i, 0)),
                  pl.BlockSpec((k, tn), lambda i, j: (0, j))],
        out_specs=[pl.BlockSpec((tm, tn), lambda i, j: (i, j)),
                   pl.BlockSpec((tm, f32_width), lambda i, j: (i, 0))],
        out_shape=[jax.ShapeDtypeStruct((n, m), BF16), jax.ShapeDtypeStruct((n, f32_width), F32)],
        compiler_params=_params("parallel", "arbitrary"),
        name="proj_in_mixed",
    )(xb, wb)


def _mm_nt_kernel(w_ref, x_ref, *rest):
    k_ref, v_ref = rest[-2:]
    kv = _dot_nt(w_ref[...], x_ref[...])
    half = k_ref.shape[0]
    k_ref[...] = kv[:half]
    v_ref[...] = kv[half:]


def _proj_kv_t(wt, xb, batch, seq, layer, prev):
    m, k = wt.shape
    w = m // 2
    tm = min(seq, 1024)
    nt = seq // tm
    out_spec = pl.BlockSpec((None, None, w, tm), lambda b, t: (layer, b, 0, t))
    out_shape = jax.ShapeDtypeStruct(prev[0].shape, F32)
    return pl.pallas_call(
        _mm_nt_kernel,
        grid=(batch, nt),
        in_specs=[pl.BlockSpec((m, k), lambda b, t: (0, 0)),
                  pl.BlockSpec((tm, k), lambda b, t: (b * nt + t, 0)),
                  pl.BlockSpec(memory_space=pl.ANY), pl.BlockSpec(memory_space=pl.ANY)],
        out_specs=[out_spec, out_spec],
        out_shape=[out_shape, out_shape],
        input_output_aliases={2: 0, 3: 1},
        compiler_params=_params("parallel", "arbitrary"),
        name="proj_kv_t",
    )(wt, xb, *prev)


def _softplus2(z):
    neg_abs = lax.bitcast_convert_type(
        lax.bitcast_convert_type(z, jnp.uint32) | jnp.uint32(0x80000000), F32)
    return jnp.maximum(z, 0.0) + jnp.log2(1.0 + jnp.exp2(neg_abs))


def _suffix_sums(sp, tri2):
    n = sp.shape[1]
    hi, lo = _split2(sp)
    c = _dot(jnp.concatenate([hi, lo], axis=1), tri2)
    return c[:, :n], c[:, n:]


def _tri_suffix2(n):
    j = np.arange(n)[:, None]
    s = np.arange(n)[None, :]
    ext = np.concatenate([(j >= s).astype(np.float32), np.ones((n, LANES), np.float32)], axis=1)
    return jnp.asarray(np.concatenate([ext, ext], axis=0), dtype=BF16)


def _sb_prompt_kernel(bias_ref, q_ref, kt_ref, vt_ref, tri_ref, o_ref, kb_ref, vb_ref, carry_ref,
                      acc_ref, *, tq, c2):
    hp = pl.program_id(1)
    qi = pl.program_id(2)
    n_blk = kb_ref.shape[0]
    sub = tri_ref.shape[0] // 2

    @pl.when(qi == 0)
    def _():
        for j in range(n_blk):
            kb_ref[j] = kt_ref[:, j * tq:(j + 1) * tq].astype(BF16)
            vb_ref[j] = vt_ref[:, j * tq:(j + 1) * tq].astype(BF16)

    lane = lax.broadcasted_iota(jnp.int32, (1, LANES), 1)
    half = LANES // 2
    q = q_ref[...].astype(F32) * c2
    qh = [jnp.where((lane >= h * half) & (lane < (h + 1) * half), q, 0.0).astype(BF16)
          for h in range(2)]
    bias2 = [bias_ref[2 * hp + h] * LOG2E for h in range(2)]
    tri2 = tri_ref[...]
    carry_ref[...] = jnp.zeros_like(carry_ref)
    acc_ref[...] = jnp.zeros_like(acc_ref)

    def logits(h, j):
        return _dot(qh[h], kb_ref[j]) + bias2[h]

    def sums_of(z, mask):
        sp = _softplus2(z)
        if mask is not None:
            sp = jnp.where(mask, sp, 0.0)
        return [_suffix_sums(sp[:, i * sub:(i + 1) * sub], tri2) for i in range(tq // sub)]

    def weigh(h, j, z, sums, mask):
        carry = carry_ref[h]
        parts = []
        for i in reversed(range(tq // sub)):
            c, tot = sums[i]
            parts.append(z[:, i * sub:(i + 1) * sub] - c - carry)
            carry = carry + tot
        carry_ref[h] = carry
        w = jnp.exp2(jnp.concatenate(parts[::-1], axis=1))
        if mask is not None:
            w = jnp.where(mask, w, 0.0)
        acc_ref[h] += _dot_nt(w.astype(BF16), vb_ref[j])

    causal = (lax.broadcasted_iota(jnp.int32, (tq, tq), 1)
              < lax.broadcasted_iota(jnp.int32, (tq, tq), 0))
    zs = [logits(h, qi) for h in range(2)]
    sums = [sums_of(z, causal) for z in zs]
    for h in range(2):
        weigh(h, qi, zs[h], sums[h], causal)

    def body(i, _):
        j = qi - 1 - i
        for h in range(2):
            z = logits(h, j)
            weigh(h, j, z, sums_of(z, None), None)
        return 0

    lax.fori_loop(0, qi, body, 0)
    o_ref[...] = jnp.where(lane < half, acc_ref[0], acc_ref[1]).astype(o_ref.dtype)


def _sb_prompt(q_src, kt, vt, layer, sb_bias, batch, seq, n_heads, d_head):
    n = q_src.shape[0]
    w_a = n_heads * d_head
    tq = SB_TQ
    nq = seq // tq
    npair = w_a // LANES
    tri2 = _tri_suffix2(LANES)
    kern = functools.partial(_sb_prompt_kernel, tq=tq, c2=LOG2E * d_head ** -0.5)
    return pl.pallas_call(
        kern,
        grid_spec=pltpu.PrefetchScalarGridSpec(
            num_scalar_prefetch=1,
            grid=(batch, npair, nq),
            in_specs=[
                pl.BlockSpec((tq, LANES), lambda b, hp, qi, bias: (b * nq + qi, hp)),
                pl.BlockSpec((None, None, LANES, seq), lambda b, hp, qi, bias: (layer, b, hp, 0)),
                pl.BlockSpec((None, None, LANES, seq), lambda b, hp, qi, bias: (layer, b, hp, 0)),
                pl.BlockSpec(tri2.shape, lambda b, hp, qi, bias: (0, 0)),
            ],
            out_specs=pl.BlockSpec((tq, LANES), lambda b, hp, qi, bias: (b * nq + qi, hp)),
            scratch_shapes=[pltpu.VMEM((nq, LANES, tq), BF16), pltpu.VMEM((nq, LANES, tq), BF16),
                            pltpu.VMEM((2, tq, LANES), F32), pltpu.VMEM((2, tq, LANES), F32)],
        ),
        out_shape=jax.ShapeDtypeStruct((n, w_a), BF16),
        compiler_params=_params("parallel", "parallel", "arbitrary"),
        name="sb_prompt",
    )(sb_bias, q_src, kt, vt, tri2)


def _sb_sample_kernel(pt_ref, q_ref, kn_ref, vn_ref, bias_ref, tri_ref, *rest,
                      n_pages, n_heads, d_head, t_new, c2):
    kpages = rest[:n_pages]
    vpages = rest[n_pages:2 * n_pages]
    o_ref = rest[2 * n_pages]
    rows = n_heads * t_new
    w_a = n_heads * d_head
    page = tri_ref.shape[0] // 2
    head_mask = (lax.broadcasted_iota(jnp.int32, (rows, w_a), 0) // t_new
                 == lax.broadcasted_iota(jnp.int32, (rows, w_a), 1) // d_head)
    bias2 = bias_ref[...] * LOG2E
    tri2 = tri_ref[...]

    q = q_ref[...] * c2
    qbd = jnp.where(head_mask, jnp.concatenate([q] * n_heads, axis=0), 0.0).astype(BF16)

    pad = jnp.zeros((page - t_new, w_a), F32)
    kn = jnp.concatenate([kn_ref[...], pad], axis=0).astype(BF16)
    vn = jnp.concatenate([vn_ref[...], pad], axis=0).astype(BF16)
    mask = (lax.broadcasted_iota(jnp.int32, (rows, page), 1)
            < lax.broadcasted_iota(jnp.int32, (rows, page), 0) % t_new)
    zs = [_dot_nt(qbd, kn) + bias2]
    zs += [_dot(qbd, kpages[r][...].astype(BF16)) + bias2 for r in range(n_pages)]
    sps = [jnp.where(mask, _softplus2(zs[0]), 0.0)] + [_softplus2(z) for z in zs[1:]]
    c_all = _dot(jnp.concatenate([jnp.concatenate(_split2(sp), axis=1) for sp in sps], axis=0), tri2)
    carry = jnp.zeros((rows, LANES), F32)
    acc = jnp.zeros((rows, w_a), F32)
    for r in range(n_pages + 1):
        c = c_all[r * rows:(r + 1) * rows]
        w = jnp.exp2(zs[r] - c[:, :page] - carry)
        carry = carry + c[:, page:]
        if r == 0:
            acc = acc + _dot(jnp.where(mask, w, 0.0).astype(BF16), vn)
        else:
            acc = acc + _dot_nt(w.astype(BF16), vpages[r - 1][...].astype(BF16))
    a = jnp.where(head_mask, acc, 0.0)
    out = a[0:t_new]
    for h in range(1, n_heads):
        out = out + a[h * t_new:(h + 1) * t_new]
    o_ref[...] = out


def _sb_sample(ys, ckt, cvt, page_table, sb_bias, layer, n_heads, d_head, t_new):
    n = ys.shape[0]
    n_seq, n_pages = page_table.shape
    page = ckt.shape[3]
    w_a = n_heads * d_head
    rows = n_heads * t_new
    bias_rep = jnp.broadcast_to(jnp.repeat(sb_bias.astype(F32), t_new)[:, None], (rows, LANES))
    tri2 = _tri_suffix2(page)

    def page_spec(r):
        return pl.BlockSpec((None, None, w_a, page),
                            lambda b, pt, r=r: (layer, pt[b, n_pages - 1 - r], 0, 0))

    kern = functools.partial(_sb_sample_kernel, n_pages=n_pages, n_heads=n_heads, d_head=d_head,
                             t_new=t_new, c2=LOG2E * d_head ** -0.5)
    return pl.pallas_call(
        kern,
        grid_spec=pltpu.PrefetchScalarGridSpec(
            num_scalar_prefetch=1,
            grid=(n_seq,),
            in_specs=[
                pl.BlockSpec((t_new, w_a), lambda b, pt: (b, 0)),
                pl.BlockSpec((t_new, w_a), lambda b, pt: (b, 1)),
                pl.BlockSpec((t_new, w_a), lambda b, pt: (b, 2)),
                pl.BlockSpec((rows, LANES), lambda b, pt: (0, 0)),
                pl.BlockSpec(tri2.shape, lambda b, pt: (0, 0)),
            ] + [page_spec(r) for r in range(n_pages)] * 2,
            out_specs=pl.BlockSpec((t_new, w_a), lambda b, pt: (b, 0)),
        ),
        out_shape=jax.ShapeDtypeStruct((n, w_a), F32),
        compiler_params=_params("arbitrary"),
        name="sb_sample",
    )(page_table, ys, ys, ys, bias_rep, tri2, *([ckt] * n_pages), *([cvt] * n_pages))


def _hgrn_masks(c, seg):
    t = np.arange(c)[:, None]
    j = np.arange(c)[None, :]
    same = (t // seg) == (j // seg)
    halves = []
    h = seg // 2
    while h >= 1:
        halves.append(h)
        h //= 2
    blocks = [same & (j <= t), same & (j > t)]
    pairs = []
    for h in halves:
        upper_t = ((t // h) % 2) == 1
        m_t = (t // h) * h
        m_next = (t // h + 1) * h
        blocks.append((upper_t & (j >= m_t) & (j <= t)) | (~upper_t & (j > t) & (j < m_next)))
        s = j
        pairs.append(upper_t & ((((s // h) % 2) == 0)) & ((t // (2 * h)) == (s // (2 * h))))
    pairs.append(t == j)
    m_all = np.concatenate(blocks, axis=0).astype(np.float32)
    m_cat = np.concatenate([m_all, m_all], axis=1)
    return (jnp.asarray(m_cat, dtype=BF16), jnp.asarray(np.stack(pairs).astype(np.float32)),
            len(halves))


def _hgrn_gates(zf, lb):
    log_sig = jnp.minimum(zf, 0.0) - jnp.log(1.0 + jnp.exp(-jnp.abs(zf)))
    a1 = jnp.log(lb)
    a2 = jnp.log(1.0 - lb) + log_sig
    g = jnp.maximum(a1, a2) + jnp.log(1.0 + jnp.exp(-jnp.abs(a1 - a2)))
    return g, (1.0 - lb) * jax.nn.sigmoid(-zf)


def _hgrn_exponents(gs, mcat):
    cols = [jnp.concatenate(_split2(g), axis=0) for g in gs]
    return _dot(mcat, cols[0] if len(cols) == 1 else jnp.concatenate(cols, axis=1))


def _hgrn_intra(q, k, v, blk, pairs, n_levels):
    vb = v.astype(BF16)
    scores = _dot_nt(q.astype(BF16), k.astype(BF16)) * pairs[n_levels]
    for i in range(n_levels):
        ex = jnp.exp(blk(2 + i))
        scores = scores + _dot_nt((q * ex).astype(BF16), (k * ex).astype(BF16)) * pairs[i]
    eb = jnp.exp(blk(0))
    kd = k * jnp.exp(blk(1))
    return _dot(scores.astype(BF16), vb), (q * eb).astype(BF16), eb.T, kd.T.astype(BF16), vb


def _hgrn_finish(o, og, norm_w):
    o = o * lax.rsqrt(jnp.mean(o * o, axis=-1, keepdims=True) + RMS_EPS) * norm_w
    return o * (og * jax.nn.sigmoid(og))


def _hgrn_prompt_kernel(q_ref, f_ref, v_ref, og_ref, lb_ref, nw_ref, mcat_ref, pairs_ref,
                        y_ref, s_out_ref, s_ref, *, n_levels, chunk):
    ti = pl.program_id(2)

    @pl.when(ti == 0)
    def _():
        s_ref[...] = jnp.zeros_like(s_ref)

    pairs = pairs_ref[...]
    n_chunks = q_ref.shape[0] // chunk
    rows = lambda a, r: a[r * chunk:(r + 1) * chunk]
    g, k = _hgrn_gates(f_ref[...], lb_ref[...])
    e = _hgrn_exponents([rows(g, r) for r in range(n_chunks)], mcat_ref[...])
    q = q_ref[...].astype(F32)
    v = v_ref[...].astype(F32)
    intra = [_hgrn_intra(rows(q, r), rows(k, r), rows(v, r),
                         lambda i, r=r: e[i * chunk:(i + 1) * chunk, r * LANES:(r + 1) * LANES],
                         pairs, n_levels) for r in range(n_chunks)]
    s = s_ref[...]
    for r, (o, qe, eb_t, kd_t, vb) in enumerate(intra):
        o = o + _dot(qe, s.astype(BF16))
        s = s * eb_t[:, chunk - 1:chunk] + _dot(kd_t, vb)
        sl = pl.ds(r * chunk, chunk)
        y_ref[sl, :] = _hgrn_finish(o, og_ref[sl, :].astype(F32), nw_ref[...]).astype(y_ref.dtype)
    s_ref[...] = s

    @pl.when(ti == pl.num_programs(2) - 1)
    def _():
        s_out_ref[...] = s


def _hgrn_sample_kernel(q_ref, f_ref, v_ref, og_ref, lb_ref, nw_ref, mcat_ref, pairs_ref,
                        s0_ref, *rest, n_levels, seg):
    y_ref, s_out_ref = rest[-2:]
    chunk = q_ref.shape[0]
    n_seq = chunk // seg
    lane_seq = lax.broadcasted_iota(jnp.int32, (1, chunk), 1) // seg
    row_seq = lax.broadcasted_iota(jnp.int32, (chunk, 1), 0) // seg
    g, k = _hgrn_gates(f_ref[...], lb_ref[...])
    e = _hgrn_exponents([g], mcat_ref[...])
    o, qe, eb_t, kd_t, vb = _hgrn_intra(q_ref[...], k, v_ref[...],
                                        lambda i: e[i * chunk:(i + 1) * chunk],
                                        pairs_ref[...], n_levels)
    for i in range(n_seq):
        s0 = s0_ref[i]
        o = o + _dot(jnp.where(row_seq == i, qe, jnp.zeros_like(qe)), s0.astype(BF16))
        last = (i + 1) * seg - 1
        kd_i = jnp.where(lane_seq == i, kd_t, jnp.zeros_like(kd_t))
        s_out_ref[i] = s0 * eb_t[:, last:last + 1] + _dot(kd_i, vb)
    y_ref[...] = _hgrn_finish(o, og_ref[...], nw_ref[...])


def _col_starts(names, sizes):
    starts, off = {}, 0
    for name in names:
        starts[name] = off
        off += sizes[name]
    return starts


def _hgrn_col_blocks(cols):
    return tuple(cols[k] // LANES for k in ("q_c", "f_c", "i_c", "og_c"))


def _hgrn_prompt(y, yf, lower, norm_w, batch, seq, n_heads_c, cols):
    n = y.shape[0]
    chunk = HG_CHUNK
    tt = min(seq, HG_TILE)
    nt = seq // tt
    mcat, pairs, n_levels = _hgrn_masks(chunk, chunk)
    cq, _, ci, cg = _hgrn_col_blocks(cols)

    def col(cb):
        return pl.BlockSpec((tt, LANES), lambda b, h, t, cb=cb: (b * nt + t, cb + h))

    const2 = lambda shape: pl.BlockSpec(shape, lambda b, h, t: (0, 0))
    kern = functools.partial(_hgrn_prompt_kernel, n_levels=n_levels, chunk=chunk)
    return pl.pallas_call(
        kern,
        grid=(batch, n_heads_c, nt),
        in_specs=[col(cq), col(0), col(ci), col(cg),
                  pl.BlockSpec((1, LANES), lambda b, h, t: (0, h)),
                  const2((1, LANES)), const2(mcat.shape),
                  pl.BlockSpec(pairs.shape, lambda b, h, t: (0, 0, 0))],
        out_specs=[pl.BlockSpec((tt, LANES), lambda b, h, t: (b * nt + t, h)),
                   pl.BlockSpec((None, None, LANES, LANES), lambda b, h, t: (b, h, 0, 0))],
        out_shape=[jax.ShapeDtypeStruct((n, n_heads_c * LANES), BF16),
                   jax.ShapeDtypeStruct((batch, n_heads_c, LANES, LANES), F32)],
        scratch_shapes=[pltpu.VMEM((LANES, LANES), F32)],
        compiler_params=_params("parallel", "parallel", "arbitrary"),
        name="hgrn_prompt",
    )(y, yf, y, y, lower, norm_w, mcat, pairs)


def _hgrn_sample(ys, lower, norm_w, state, layer, t_new, n_heads_c, cols, prev):
    n = ys.shape[0]
    chunk = HG_CHUNK
    g = chunk // t_new
    mcat, pairs, n_levels = _hgrn_masks(chunk, t_new)
    cq, cf, ci, cg = _hgrn_col_blocks(cols)

    def col(cb):
        return pl.BlockSpec((chunk, LANES), lambda i, h, cb=cb: (i, cb + h))

    const2 = lambda shape: pl.BlockSpec(shape, lambda i, h: (0, 0))
    state_spec = pl.BlockSpec((None, g, None, LANES, LANES), lambda i, h: (layer, i, h, 0, 0))
    in_specs = [col(cq), col(cf), col(ci), col(cg),
                pl.BlockSpec((1, LANES), lambda i, h: (0, h)),
                const2((1, LANES)), const2(mcat.shape),
                pl.BlockSpec(pairs.shape, lambda i, h: (0, 0, 0)), state_spec,
                pl.BlockSpec(memory_space=pl.ANY)]
    args = [ys, ys, ys, ys, lower, norm_w, mcat, pairs, state, prev]
    aliases = {len(args) - 1: 1}
    kern = functools.partial(_hgrn_sample_kernel, n_levels=n_levels, seg=t_new)
    return pl.pallas_call(
        kern,
        grid=(n // chunk, n_heads_c),
        in_specs=in_specs,
        out_specs=[pl.BlockSpec((chunk, LANES), lambda i, h: (i, h)), state_spec],
        out_shape=[jax.ShapeDtypeStruct((n, n_heads_c * LANES), F32),
                   jax.ShapeDtypeStruct(state.shape, F32)],
        input_output_aliases=aliases,
        compiler_params=_params("parallel", "parallel"),
        name="hgrn_sample",
    )(*args)


def _conv_prompt(u, w, carry_ref):
    tm = u.shape[0]
    r = lax.broadcasted_iota(jnp.int32, (tm, 1), 0)
    c6 = carry_ref[SUBLANES - 2:SUBLANES - 1, :]
    c7 = carry_ref[SUBLANES - 1:SUBLANES, :]
    u1 = jnp.where(r == 0, c7, pltpu.roll(u, 1, 0))
    u2 = jnp.where(r == 0, c6, jnp.where(r == 1, c7, pltpu.roll(u, 2, 0)))
    carry_ref[...] = u[tm - SUBLANES:, :]
    return w[2:3, :] * u + w[1:2, :] * u1 + w[0:1, :] * u2


def _conv_sample(u, w, hist, t_new):
    tm = u.shape[0]
    pos = lax.broadcasted_iota(jnp.int32, (tm, 1), 0) % t_new
    u1 = jnp.where(pos == 0, pltpu.roll(hist, tm - 1, 0), pltpu.roll(u, 1, 0))
    u2 = jnp.where(pos < 2, hist, pltpu.roll(u, 2, 0))
    return w[2:3, :] * u + w[1:2, :] * u1 + w[0:1, :] * u2


def _layer_norm(h, g, b):
    hc = h - jnp.mean(h, axis=-1, keepdims=True)
    var = jnp.mean(hc * hc, axis=-1, keepdims=True)
    return hc * lax.rsqrt(var + LN_EPS) * g + b


def _merge_kernel(x_ref, gb_ref, gc_ref, hb_ref, ma_ref, mb_ref, mc_ref, ya_ref, yc_ref,
                  cw_ref, wa_ref, wb_ref, wc_ref, wo_ref, g_ref, b_ref, *rest,
                  sample, t_new, tiles_per_seq, alpha):
    if sample:
        hist_ref, x1_ref, x1b_ref, u_ref = rest
    else:
        x1_ref, x1b_ref, tail_ref, carry_ref = rest

        @pl.when(pl.program_id(0) % tiles_per_seq == 0)
        def _():
            carry_ref[...] = jnp.zeros_like(carry_ref)

    u = gc_ref[...].astype(F32) * hb_ref[...].astype(F32)
    cw = cw_ref[...]
    if sample:
        conv = _conv_sample(u, cw, hist_ref[...], t_new)
        u_ref[...] = u
    else:
        conv = _conv_prompt(u, cw, carry_ref)
        tail_ref[...] = u[u.shape[0] - SUBLANES:, :]
    yb = gb_ref[...].astype(F32) * conv
    gate = lambda ref: jax.nn.sigmoid(ref[...].astype(F32))
    mix = (gate(ma_ref) * _dot(ya_ref[...].astype(BF16), wa_ref[...])
           + gate(mb_ref) * _dot(yb.astype(BF16), wb_ref[...])
           + gate(mc_ref) * _dot(yc_ref[...].astype(BF16), wc_ref[...]))
    h = alpha * x_ref[...] + _dot(mix.astype(BF16), wo_ref[...])
    x1 = _layer_norm(h, g_ref[...], b_ref[...])
    x1_ref[...] = x1
    x1b_ref[...] = x1.astype(BF16)


def _merge(x, y, ya, yc, conv_w, wa, wb, wc, wo, ln_g, ln_b, alpha, cols, seq, hist=None,
           t_new=None):
    n, d = x.shape
    w_b = conv_w.shape[1]
    sample = hist is not None
    tm = min(n, MERGE_TM)
    nt = n // tm
    off_b = cols["gate_b"] // w_b
    off_m = cols["m_a"] // d
    row = lambda width, cb: pl.BlockSpec((tm, width), lambda i, cb=cb: (i, cb))
    const = lambda a: pl.BlockSpec(a.shape, lambda i: (0, 0))
    in_specs = [row(d, 0), row(w_b, off_b), row(w_b, off_b + 1), row(w_b, off_b + 2),
                row(d, off_m), row(d, off_m + 1), row(d, off_m + 2),
                row(ya.shape[1], 0), row(yc.shape[1], 0),
                const(conv_w), const(wa), const(wb), const(wc), const(wo), const(ln_g), const(ln_b)]
    args = [x, y, y, y, y, y, y, ya, yc, conv_w, wa, wb, wc, wo, ln_g, ln_b]
    out_specs = [row(d, 0), row(d, 0)]
    out_shape = [jax.ShapeDtypeStruct((n, d), F32), jax.ShapeDtypeStruct((n, d), BF16)]
    scratch = []
    if sample:
        in_specs.append(row(w_b, 0))
        args.append(hist)
        out_specs.append(row(w_b, 0))
        out_shape.append(jax.ShapeDtypeStruct((n, w_b), F32))
    else:
        out_specs.append(pl.BlockSpec((SUBLANES, w_b), lambda i: (i, 0)))
        out_shape.append(jax.ShapeDtypeStruct((nt * SUBLANES, w_b), F32))
        scratch.append(pltpu.VMEM((SUBLANES, w_b), F32))
    kern = functools.partial(_merge_kernel, sample=sample, t_new=t_new,
                             tiles_per_seq=max(seq // tm, 1), alpha=alpha)
    return pl.pallas_call(
        kern, grid=(nt,), in_specs=in_specs, out_specs=out_specs, out_shape=out_shape,
        scratch_shapes=scratch, compiler_params=_params("arbitrary"),
        name="merge_sample" if sample else "merge_prompt",
    )(*args)


def _ffn_kernel(xb_ref, x_ref, wu_ref, cw_ref, cb_ref, wd_ref, g_ref, b_ref, *rest,
                sample, t_new, tiles_per_seq, alpha):
    if sample:
        hist_ref, o_ref, ob_ref, a_ref = rest
    else:
        o_ref, ob_ref, tail_ref, carry_ref = rest
    d_ff = wd_ref.shape[0]
    if not sample:
        @pl.when(pl.program_id(0) % tiles_per_seq == 0)
        def _():
            carry_ref[...] = jnp.zeros_like(carry_ref)

    xb = xb_ref[...]
    y = jnp.zeros(x_ref.shape, F32)
    chunk = min(d_ff, FFN_CHUNK)
    for c in range(d_ff // chunk):
        lo, hi = c * chunk, (c + 1) * chunk
        a = _dot(xb, wu_ref[:, lo:hi])
        gate = _dot(xb, wu_ref[:, d_ff + lo:d_ff + hi])
        cw = cw_ref[:, lo:hi]
        if sample:
            conv = _conv_sample(a, cw, hist_ref[:, lo:hi], t_new)
            a_ref[:, lo:hi] = a
        else:
            conv = _conv_prompt(a, cw, carry_ref.at[:, lo:hi])
            tail_ref[:, lo:hi] = a[a.shape[0] - SUBLANES:, :]
        h = jax.nn.gelu(conv + cb_ref[:, lo:hi]) * gate
        y = y + _dot(h.astype(BF16), wd_ref[lo:hi, :])
    x2 = _layer_norm(alpha * x_ref[...] + y, g_ref[...], b_ref[...])
    o_ref[...] = x2
    ob_ref[...] = x2.astype(BF16)


def _ffn(xb, x, w_up, conv_w, conv_b, w_down, ln_g, ln_b, alpha, seq, hist=None, t_new=None):
    n, d = x.shape
    d_ff = w_down.shape[0]
    sample = hist is not None
    tm = min(n, FFN_TM // 2 if sample else FFN_TM)
    nt = n // tm
    row = lambda width: pl.BlockSpec((tm, width), lambda i: (i, 0))
    const = lambda a: pl.BlockSpec(a.shape, lambda i: (0, 0), pipeline_mode=pl.Buffered(1))
    in_specs = [row(d), row(d), const(w_up), const(conv_w), const(conv_b), const(w_down),
                const(ln_g), const(ln_b)]
    args = [xb, x, w_up, conv_w, conv_b, w_down, ln_g, ln_b]
    out_specs = [row(d), row(d)]
    out_shape = [jax.ShapeDtypeStruct((n, d), F32), jax.ShapeDtypeStruct((n, d), BF16)]
    scratch = []
    if sample:
        in_specs.append(row(d_ff))
        args.append(hist)
        out_specs.append(row(d_ff))
        out_shape.append(jax.ShapeDtypeStruct((n, d_ff), F32))
    else:
        out_specs.append(pl.BlockSpec((SUBLANES, d_ff), lambda i: (i, 0)))
        out_shape.append(jax.ShapeDtypeStruct((nt * SUBLANES, d_ff), F32))
        scratch.append(pltpu.VMEM((SUBLANES, d_ff), F32))
    kern = functools.partial(_ffn_kernel, sample=sample, t_new=t_new,
                             tiles_per_seq=max(seq // tm, 1), alpha=alpha)
    return pl.pallas_call(
        kern, grid=(nt,), in_specs=in_specs, out_specs=out_specs, out_shape=out_shape,
        scratch_shapes=scratch, compiler_params=_params("arbitrary"),
        name="ffn_sample" if sample else "ffn_prompt",
    )(*args)


def _pad_hist(state, t_new):
    n_seq, nb, c = state.shape
    return jnp.pad(state, ((0, 0), (0, t_new - nb), (0, 0))).reshape(n_seq * t_new, c)


def _tails(tail, batch, tiles_per_seq):
    c = tail.shape[1]
    t = tail.reshape(batch, tiles_per_seq, SUBLANES, c)
    return t[:, tiles_per_seq - 1, SUBLANES - 2:, :]


def kernel(x_prompt, x_sample, cache_k, cache_v, state_conv, state_hgrn, state_ffn_conv, page_table,
           w_in, sb_bias, conv_w, hgrn_lb, hgrn_norm_w, w_br_a, w_br_b, w_br_c, w_o, ln1_g, ln1_b,
           w_up, ffn_conv_w, ffn_conv_b, w_down, ln2_g, ln2_b):
    depth, d_model, d_in = w_in.shape
    batch, seq, _ = x_prompt.shape
    n_seq, t_new, _ = x_sample.shape
    n_pool, page, n_heads, d_head = cache_k.shape[1:]
    w_a = n_heads * d_head
    w_b = conv_w.shape[2]
    n_heads_c, dk_c, dv_c = state_hgrn.shape[2:]
    d_ff = w_down.shape[1]
    names = ("q_a", "k_a", "v_a", "gate_b", "gate_c", "h_b", "q_c", "f_c", "i_c", "og_c",
             "m_a", "m_b", "m_c")
    widths = (w_a, w_a, w_a, w_b, w_b, w_b, n_heads_c * dk_c, n_heads_c * dk_c, n_heads_c * dv_c,
              n_heads_c * dv_c, d_model, d_model, d_model)
    sizes = dict(zip(names, widths))
    assert sum(widths) == d_in and dk_c == LANES and dv_c == LANES and 2 * d_head == LANES
    cols_s = _col_starts(names, sizes)
    cols_p = _col_starts([k for k in names if k not in ("k_a", "v_a")], sizes)
    alpha = (2 * depth) ** 0.25

    lbs = jnp.cumsum(jax.nn.softmax(hgrn_lb.astype(F32), axis=0), axis=0)
    lower = lbs - lbs[0]

    ckt = jnp.transpose(cache_k, (0, 1, 3, 4, 2)).reshape(depth, n_pool, w_a, page)
    cvt = jnp.transpose(cache_v, (0, 1, 3, 4, 2)).reshape(depth, n_pool, w_a, page)
    bf = lambda a: a.astype(BF16)
    w_in_b, wa_b, wb_b, wc_b, wo_b, wu_b, wd_b = map(bf, (w_in, w_br_a, w_br_b, w_br_c, w_o, w_up, w_down))
    kv0, kv1 = cols_s["k_a"], cols_s["gate_b"]
    w_rest_b = jnp.concatenate([w_in_b[:, :, :kv0], w_in_b[:, :, kv1:]], axis=2)
    w_kvt_b = jnp.transpose(w_in_b[:, :, kv0:kv1], (0, 2, 1))

    hp = x_prompt.reshape(batch * seq, d_model)
    hs = x_sample.reshape(n_seq * t_new, d_model)
    hpb, hsb = bf(hp), bf(hs)
    outs = {k: [] for k in ("conv_p", "s_p", "ffn_p", "k_s", "v_s", "conv_s", "ffn_s")}
    kv_all = [jnp.zeros((depth, batch, w_a, seq), F32) for _ in range(2)]
    s_all = jnp.zeros(state_hgrn.shape, F32)
    for l in range(depth):
        row = lambda a: a[l].reshape(1, -1)
        yp, yf = _proj_in_mixed(hpb, w_rest_b[l], cols_p["f_c"], sizes["f_c"])
        kv_all = _proj_kv_t(w_kvt_b[l], hpb, batch, seq, l, kv_all)
        ys = _proj_in(hsb, w_in_b[l])
        ya_p = _sb_prompt(yp, *kv_all, l, sb_bias[l], batch, seq, n_heads, d_head)
        ya_s = _sb_sample(ys, ckt, cvt, page_table, sb_bias[l], l, n_heads, d_head, t_new)
        yc_p, s_p = _hgrn_prompt(yp, yf, row(lower), row(hgrn_norm_w), batch, seq, n_heads_c, cols_p)
        yc_s, s_all = _hgrn_sample(ys, row(lower), row(hgrn_norm_w), state_hgrn, l, t_new, n_heads_c,
                                   cols_s, s_all)
        margs = (conv_w[l], wa_b[l], wb_b[l], wc_b[l], wo_b[l], row(ln1_g), row(ln1_b), alpha)
        x1p, x1pb, tail_b = _merge(hp, yp, ya_p, yc_p, *margs, cols_p, seq)
        x1s, x1sb, u_s = _merge(hs, ys, ya_s, yc_s, *margs, cols_s, t_new,
                                hist=_pad_hist(state_conv[l], t_new), t_new=t_new)
        fargs = (wu_b[l], ffn_conv_w[l], row(ffn_conv_b), wd_b[l], row(ln2_g), row(ln2_b), alpha)
        hp, hpb, tail_f = _ffn(x1pb, x1p, *fargs, seq)
        hs, hsb, a_s = _ffn(x1sb, x1s, *fargs, t_new,
                            hist=_pad_hist(state_ffn_conv[l], t_new), t_new=t_new)

        outs["conv_p"].append(_tails(tail_b, batch, seq // min(batch * seq, MERGE_TM)))
        outs["s_p"].append(s_p)
        outs["ffn_p"].append(_tails(tail_f, batch, seq // min(batch * seq, FFN_TM)))
        outs["k_s"].append(ys[:, kv0:kv0 + w_a].reshape(n_seq, t_new, n_heads, d_head))
        outs["v_s"].append(ys[:, kv0 + w_a:kv1].reshape(n_seq, t_new, n_heads, d_head))
        outs["conv_s"].append(u_s.reshape(n_seq, t_new, w_b)[:, t_new - 2:])
        outs["ffn_s"].append(a_s.reshape(n_seq, t_new, d_ff)[:, t_new - 2:])

    def heads_t(a):
        return jnp.transpose(a.reshape(depth, batch, n_heads, d_head, seq), (0, 1, 4, 2, 3))

    st = {k: jnp.stack(v) for k, v in outs.items()}
    return (hp.reshape(batch, seq, d_model), hs.reshape(n_seq, t_new, d_model),
            heads_t(kv_all[0]), heads_t(kv_all[1]), st["conv_p"], st["s_p"], st["ffn_p"],
            st["k_s"], st["v_s"], st["conv_s"], s_all, st["ffn_s"])
```

```python
import functools
import math

import numpy as np
import jax
import jax.numpy as jnp
from jax import lax
from jax.experimental import pallas as pl
from jax.experimental.pallas import tpu as pltpu

F32 = jnp.float32
BF16 = jnp.bfloat16

LANES = 128
SUBLANES = 8
CONV_TAPS = 3
LN_EPS = 1e-5
RMS_EPS = 1e-6
LOG2E = math.log2(math.e)
HG_CHUNK = 128
HG_TILE = 2048
MERGE_TM = 512
FFN_TM = 512
SB_TQ = 512
FFN_CHUNK = 2816
VMEM_LIMIT = 56 * 1024 * 1024


def _params(*sem):
    return pltpu.CompilerParams(dimension_semantics=sem, vmem_limit_bytes=VMEM_LIMIT)


def _split2(x):
    hi = lax.bitcast_convert_type(
        lax.bitcast_convert_type(x, jnp.uint32) & jnp.uint32(0xFFFF0000), F32)
    return hi.astype(BF16), (x - hi).astype(BF16)


def _dot(a, b):
    return jnp.dot(a, b, preferred_element_type=F32)


def _dot_nt(a, b):
    return lax.dot_general(a, b, (((1,), (1,)), ((), ())), preferred_element_type=F32)


def _mm_kernel(x_ref, w_ref, o_ref):
    o_ref[...] = _dot(x_ref[...], w_ref[...])


def _proj_in(xb, wb):
    n, k = xb.shape
    m = wb.shape[1]
    tm = min(n, 1024)
    tn = 1024
    return pl.pallas_call(
        _mm_kernel,
        grid=(n // tm, m // tn),
        in_specs=[pl.BlockSpec((tm, k), lambda i, j: (i, 0)),
                  pl.BlockSpec((k, tn), lambda i, j: (0, j))],
        out_specs=pl.BlockSpec((tm, tn), lambda i, j: (i, j)),
        out_shape=jax.ShapeDtypeStruct((n, m), F32),
        compiler_params=_params("parallel", "arbitrary"),
        name="proj_in",
    )(xb, wb)


def _mm_mixed_kernel(x_ref, w_ref, o_ref, f_ref, *, jf, off):
    acc = _dot(x_ref[...], w_ref[...])
    o_ref[...] = acc.astype(BF16)

    @pl.when(pl.program_id(1) == jf)
    def _():
        f_ref[...] = acc[:, off:off + f_ref.shape[1]]


def _proj_in_mixed(xb, wb, f32_start, f32_width):
    n, k = xb.shape
    m = wb.shape[1]
    tm = min(n, 1024)
    tn = 1024
    jf, off = divmod(f32_start, tn)
    assert off + f32_width <= tn
    return pl.pallas_call(
        functools.partial(_mm_mixed_kernel, jf=jf, off=off),
        grid=(n // tm, m // tn),
        in_specs=[pl.BlockSpec((tm, k), lambda i, j: (i, 0)),
                  pl.BlockSpec((k, tn), lambda i, j: (0, j))],
        out_specs=[pl.BlockSpec((tm, tn), lambda i, j: (i, j)),
                   pl.BlockSpec((tm, f32_width), lambda i, j: (i, 0))],
        out_shape=[jax.ShapeDtypeStruct((n, m), BF16), jax.ShapeDtypeStruct((n, f32_width), F32)],
        compiler_params=_params("parallel", "arbitrary"),
        name="proj_in_mixed",
    )(xb, wb)


def _mm_nt_kernel(w_ref, x_ref, *rest):
    k_ref, v_ref = rest[-2:]
    kv = _dot_nt(w_ref[...], x_ref[...])
    half = k_ref.shape[0]
    k_ref[...] = kv[:half]
    v_ref[...] = kv[half:]


def _proj_kv_t(wt, xb, batch, seq, layer, prev):
    m, k = wt.shape
    w = m // 2
    tm = min(seq, 1024)
    nt = seq // tm
    out_spec = pl.BlockSpec((None, None, w, tm), lambda b, t: (layer, b, 0, t))
    out_shape = jax.ShapeDtypeStruct(prev[0].shape, F32)
    return pl.pallas_call(
        _mm_nt_kernel,
        grid=(batch, nt),
        in_specs=[pl.BlockSpec((m, k), lambda b, t: (0, 0)),
                  pl.BlockSpec((tm, k), lambda b, t: (b * nt + t, 0)),
                  pl.BlockSpec(memory_space=pl.ANY), pl.BlockSpec(memory_space=pl.ANY)],
        out_specs=[out_spec, out_spec],
        out_shape=[out_shape, out_shape],
        input_output_aliases={2: 0, 3: 1},
        compiler_params=_params("parallel", "arbitrary"),
        name="proj_kv_t",
    )(wt, xb, *prev)


def _softplus2(z):
    neg_abs = lax.bitcast_convert_type(
        lax.bitcast_convert_type(z, jnp.uint32) | jnp.uint32(0x80000000), F32)
    return jnp.maximum(z, 0.0) + jnp.log2(1.0 + jnp.exp2(neg_abs))


def _suffix_sums(sp, tri2):
    n = sp.shape[1]
    hi, lo = _split2(sp)
    c = _dot(jnp.concatenate([hi, lo], axis=1), tri2)
    return c[:, :n], c[:, n:]


def _tri_suffix2(n):
    j = np.arange(n)[:, None]
    s = np.arange(n)[None, :]
    ext = np.concatenate([(j >= s).astype(np.float32), np.ones((n, LANES), np.float32)], axis=1)
    return jnp.asarray(np.concatenate([ext, ext], axis=0), dtype=BF16)


def _sb_prompt_kernel(bias_ref, q_ref, kt_ref, vt_ref, tri_ref, o_ref, kb_ref, vb_ref, carry_ref,
                      acc_ref, *, tq, c2):
    hp = pl.program_id(1)
    qi = pl.program_id(2)
    n_blk = kb_ref.shape[0]
    sub = tri_ref.shape[0] // 2

    @pl.when(qi == 0)
    def _():
        for j in range(n_blk):
            kb_ref[j] = kt_ref[:, j * tq:(j + 1) * tq].astype(BF16)
            vb_ref[j] = vt_ref[:, j * tq:(j + 1) * tq].astype(BF16)

    lane = lax.broadcasted_iota(jnp.int32, (1, LANES), 1)
    half = LANES // 2
    q = q_ref[...].astype(F32) * c2
    qh = [jnp.where((lane >= h * half) & (lane < (h + 1) * half), q, 0.0).astype(BF16)
          for h in range(2)]
    bias2 = [bias_ref[2 * hp + h] * LOG2E for h in range(2)]
    tri2 = tri_ref[...]
    carry_ref[...] = jnp.zeros_like(carry_ref)
    acc_ref[...] = jnp.zeros_like(acc_ref)

    def logits(h, j):
        return _dot(qh[h], kb_ref[j]) + bias2[h]

    def sums_of(z, mask):
        sp = _softplus2(z)
        if mask is not None:
            sp = jnp.where(mask, sp, 0.0)
        return [_suffix_sums(sp[:, i * sub:(i + 1) * sub], tri2) for i in range(tq // sub)]

    def weigh(h, j, z, sums, mask):
        carry = carry_ref[h]
        parts = []
        for i in reversed(range(tq // sub)):
            c, tot = sums[i]
            parts.append(z[:, i * sub:(i + 1) * sub] - c - carry)
            carry = carry + tot
        carry_ref[h] = carry
        w = jnp.exp2(jnp.concatenate(parts[::-1], axis=1))
        if mask is not None:
            w = jnp.where(mask, w, 0.0)
        acc_ref[h] += _dot_nt(w.astype(BF16), vb_ref[j])

    causal = (lax.broadcasted_iota(jnp.int32, (tq, tq), 1)
              < lax.broadcasted_iota(jnp.int32, (tq, tq), 0))
    zs = [logits(h, qi) for h in range(2)]
    sums = [sums_of(z, causal) for z in zs]
    for h in range(2):
        weigh(h, qi, zs[h], sums[h], causal)

    def body(i, _):
        j = qi - 1 - i
        for h in range(2):
            z = logits(h, j)
            weigh(h, j, z, sums_of(z, None), None)
        return 0

    lax.fori_loop(0, qi, body, 0)
    o_ref[...] = jnp.where(lane < half, acc_ref[0], acc_ref[1]).astype(o_ref.dtype)


def _sb_prompt(q_src, kt, vt, layer, sb_bias, batch, seq, n_heads, d_head):
    n = q_src.shape[0]
    w_a = n_heads * d_head
    tq = SB_TQ
    nq = seq // tq
    npair = w_a // LANES
    tri2 = _tri_suffix2(LANES)
    kern = functools.partial(_sb_prompt_kernel, tq=tq, c2=LOG2E * d_head ** -0.5)
    return pl.pallas_call(
        kern,
        grid_spec=pltpu.PrefetchScalarGridSpec(
            num_scalar_prefetch=1,
            grid=(batch, npair, nq),
            in_specs=[
                pl.BlockSpec((tq, LANES), lambda b, hp, qi, bias: (b * nq + qi, hp)),
                pl.BlockSpec((None, None, LANES, seq), lambda b, hp, qi, bias: (layer, b, hp, 0)),
                pl.BlockSpec((None, None, LANES, seq), lambda b, hp, qi, bias: (layer, b, hp, 0)),
                pl.BlockSpec(tri2.shape, lambda b, hp, qi, bias: (0, 0)),
            ],
            out_specs=pl.BlockSpec((tq, LANES), lambda b, hp, qi, bias: (b * nq + qi, hp)),
            scratch_shapes=[pltpu.VMEM((nq, LANES, tq), BF16), pltpu.VMEM((nq, LANES, tq), BF16),
                            pltpu.VMEM((2, tq, LANES), F32), pltpu.VMEM((2, tq, LANES), F32)],
        ),
        out_shape=jax.ShapeDtypeStruct((n, w_a), BF16),
        compiler_params=_params("parallel", "parallel", "arbitrary"),
        name="sb_prompt",
    )(sb_bias, q_src, kt, vt, tri2)


def _sb_sample_kernel(pt_ref, q_ref, kn_ref, vn_ref, bias_ref, tri_ref, *rest,
                      n_pages, n_heads, d_head, t_new, c2):
    kpages = rest[:n_pages]
    vpages = rest[n_pages:2 * n_pages]
    o_ref = rest[2 * n_pages]
    rows = n_heads * t_new
    w_a = n_heads * d_head
    page = tri_ref.shape[0] // 2
    head_mask = (lax.broadcasted_iota(jnp.int32, (rows, w_a), 0) // t_new
                 == lax.broadcasted_iota(jnp.int32, (rows, w_a), 1) // d_head)
    bias2 = bias_ref[...] * LOG2E
    tri2 = tri_ref[...]

    q = q_ref[...] * c2
    qbd = jnp.where(head_mask, jnp.concatenate([q] * n_heads, axis=0), 0.0).astype(BF16)

    pad = jnp.zeros((page - t_new, w_a), F32)
    kn = jnp.concatenate([kn_ref[...], pad], axis=0).astype(BF16)
    vn = jnp.concatenate([vn_ref[...], pad], axis=0).astype(BF16)
    mask = (lax.broadcasted_iota(jnp.int32, (rows, page), 1)
            < lax.broadcasted_iota(jnp.int32, (rows, page), 0) % t_new)
    zs = [_dot_nt(qbd, kn) + bias2]
    zs += [_dot(qbd, kpages[r][...].astype(BF16)) + bias2 for r in range(n_pages)]
    sps = [jnp.where(mask, _softplus2(zs[0]), 0.0)] + [_softplus2(z) for z in zs[1:]]
    c_all = _dot(jnp.concatenate([jnp.concatenate(_split2(sp), axis=1) for sp in sps], axis=0), tri2)
    carry = jnp.zeros((rows, LANES), F32)
    acc = jnp.zeros((rows, w_a), F32)
    for r in range(n_pages + 1):
        c = c_all[r * rows:(r + 1) * rows]
        w = jnp.exp2(zs[r] - c[:, :page] - carry)
        carry = carry + c[:, page:]
        if r == 0:
            acc = acc + _dot(jnp.where(mask, w, 0.0).astype(BF16), vn)
        else:
            acc = acc + _dot_nt(w.astype(BF16), vpages[r - 1][...].astype(BF16))
    a = jnp.where(head_mask, acc, 0.0)
    out = a[0:t_new]
    for h in range(1, n_heads):
        out = out + a[h * t_new:(h + 1) * t_new]
    o_ref[...] = out


def _sb_sample(ys, ckt, cvt, page_table, sb_bias, layer, n_heads, d_head, t_new):
    n = ys.shape[0]
    n_seq, n_pages = page_table.shape
    page = ckt.shape[3]
    w_a = n_heads * d_head
    rows = n_heads * t_new
    bias_rep = jnp.broadcast_to(jnp.repeat(sb_bias.astype(F32), t_new)[:, None], (rows, LANES))
    tri2 = _tri_suffix2(page)

    def page_spec(r):
        return pl.BlockSpec((None, None, w_a, page),
                            lambda b, pt, r=r: (layer, pt[b, n_pages - 1 - r], 0, 0))

    kern = functools.partial(_sb_sample_kernel, n_pages=n_pages, n_heads=n_heads, d_head=d_head,
                             t_new=t_new, c2=LOG2E * d_head ** -0.5)
    return pl.pallas_call(
        kern,
        grid_spec=pltpu.PrefetchScalarGridSpec(
            num_scalar_prefetch=1,
            grid=(n_seq,),
            in_specs=[
                pl.BlockSpec((t_new, w_a), lambda b, pt: (b, 0)),
                pl.BlockSpec((t_new, w_a), lambda b, pt: (b, 1)),
                pl.BlockSpec((t_new, w_a), lambda b, pt: (b, 2)),
                pl.BlockSpec((rows, LANES), lambda b, pt: (0, 0)),
                pl.BlockSpec(tri2.shape, lambda b, pt: (0, 0)),
            ] + [page_spec(r) for r in range(n_pages)] * 2,
            out_specs=pl.BlockSpec((t_new, w_a), lambda b, pt: (b, 0)),
        ),
        out_shape=jax.ShapeDtypeStruct((n, w_a), F32),
        compiler_params=_params("arbitrary"),
        name="sb_sample",
    )(page_table, ys, ys, ys, bias_rep, tri2, *([ckt] * n_pages), *([cvt] * n_pages))


def _hgrn_masks(c, seg):
    t = np.arange(c)[:, None]
    j = np.arange(c)[None, :]
    same = (t // seg) == (j // seg)
    halves = []
    h = seg // 2
    while h >= 1:
        halves.append(h)
        h //= 2
    blocks = [same & (j <= t), same & (j > t)]
    pairs = []
    for h in halves:
        upper_t = ((t // h) % 2) == 1
        m_t = (t // h) * h
        m_next = (t // h + 1) * h
        blocks.append((upper_t & (j >= m_t) & (j <= t)) | (~upper_t & (j > t) & (j < m_next)))
        s = j
        pairs.append(upper_t & ((((s // h) % 2) == 0)) & ((t // (2 * h)) == (s // (2 * h))))
    pairs.append(t == j)
    m_all = np.concatenate(blocks, axis=0).astype(np.float32)
    m_cat = np.concatenate([m_all, m_all], axis=1)
    return (jnp.asarray(m_cat, dtype=BF16), jnp.asarray(np.stack(pairs).astype(np.float32)),
            len(halves))


def _hgrn_gates(zf, lb):
    log_sig = jnp.minimum(zf, 0.0) - jnp.log(1.0 + jnp.exp(-jnp.abs(zf)))
    a1 = jnp.log(lb)
    a2 = jnp.log(1.0 - lb) + log_sig
    g = jnp.maximum(a1, a2) + jnp.log(1.0 + jnp.exp(-jnp.abs(a1 - a2)))
    return g, (1.0 - lb) * jax.nn.sigmoid(-zf)


def _hgrn_exponents(gs, mcat):
    cols = [jnp.concatenate(_split2(g), axis=0) for g in gs]
    return _dot(mcat, cols[0] if len(cols) == 1 else jnp.concatenate(cols, axis=1))


def _hgrn_intra(q, k, v, blk, pairs, n_levels):
    vb = v.astype(BF16)
    scores = _dot_nt(q.astype(BF16), k.astype(BF16)) * pairs[n_levels]
    for i in range(n_levels):
        ex = jnp.exp(blk(2 + i))
        scores = scores + _dot_nt((q * ex).astype(BF16), (k * ex).astype(BF16)) * pairs[i]
    eb = jnp.exp(blk(0))
    kd = k * jnp.exp(blk(1))
    return _dot(scores.astype(BF16), vb), (q * eb).astype(BF16), eb.T, kd.T.astype(BF16), vb


def _hgrn_finish(o, og, norm_w):
    o = o * lax.rsqrt(jnp.mean(o * o, axis=-1, keepdims=True) + RMS_EPS) * norm_w
    return o * (og * jax.nn.sigmoid(og))


def _hgrn_prompt_kernel(q_ref, f_ref, v_ref, og_ref, lb_ref, nw_ref, mcat_ref, pairs_ref,
                        y_ref, s_out_ref, s_ref, *, n_levels, chunk):
    ti = pl.program_id(2)

    @pl.when(ti == 0)
    def _():
        s_ref[...] = jnp.zeros_like(s_ref)

    pairs = pairs_ref[...]
    n_chunks = q_ref.shape[0] // chunk
    rows = lambda a, r: a[r * chunk:(r + 1) * chunk]
    g, k = _hgrn_gates(f_ref[...], lb_ref[...])
    e = _hgrn_exponents([rows(g, r) for r in range(n_chunks)], mcat_ref[...])
    q = q_ref[...].astype(F32)
    v = v_ref[...].astype(F32)
    intra = [_hgrn_intra(rows(q, r), rows(k, r), rows(v, r),
                         lambda i, r=r: e[i * chunk:(i + 1) * chunk, r * LANES:(r + 1) * LANES],
                         pairs, n_levels) for r in range(n_chunks)]
    s = s_ref[...]
    for r, (o, qe, eb_t, kd_t, vb) in enumerate(intra):
        o = o + _dot(qe, s.astype(BF16))
        s = s * eb_t[:, chunk - 1:chunk] + _dot(kd_t, vb)
        sl = pl.ds(r * chunk, chunk)
        y_ref[sl, :] = _hgrn_finish(o, og_ref[sl, :].astype(F32), nw_ref[...]).astype(y_ref.dtype)
    s_ref[...] = s

    @pl.when(ti == pl.num_programs(2) - 1)
    def _():
        s_out_ref[...] = s


def _hgrn_sample_kernel(q_ref, f_ref, v_ref, og_ref, lb_ref, nw_ref, mcat_ref, pairs_ref,
                        s0_ref, *rest, n_levels, seg):
    y_ref, s_out_ref = rest[-2:]
    chunk = q_ref.shape[0]
    n_seq = chunk // seg
    lane_seq = lax.broadcasted_iota(jnp.int32, (1, chunk), 1) // seg
    row_seq = lax.broadcasted_iota(jnp.int32, (chunk, 1), 0) // seg
    g, k = _hgrn_gates(f_ref[...], lb_ref[...])
    e = _hgrn_exponents([g], mcat_ref[...])
    o, qe, eb_t, kd_t, vb = _hgrn_intra(q_ref[...], k, v_ref[...],
                                        lambda i: e[i * chunk:(i + 1) * chunk],
                                        pairs_ref[...], n_levels)
    for i in range(n_seq):
        s0 = s0_ref[i]
        o = o + _dot(jnp.where(row_seq == i, qe, jnp.zeros_like(qe)), s0.astype(BF16))
        last = (i + 1) * seg - 1
        kd_i = jnp.where(lane_seq == i, kd_t, jnp.zeros_like(kd_t))
        s_out_ref[i] = s0 * eb_t[:, last:last + 1] + _dot(kd_i, vb)
    y_ref[...] = _hgrn_finish(o, og_ref[...], nw_ref[...])


def _col_starts(names, sizes):
    starts, off = {}, 0
    for name in names:
        starts[name] = off
        off += sizes[name]
    return starts


def _hgrn_col_blocks(cols):
    return tuple(cols[k] // LANES for k in ("q_c", "f_c", "i_c", "og_c"))


def _hgrn_prompt(y, yf, lower, norm_w, batch, seq, n_heads_c, cols):
    n = y.shape[0]
    chunk = HG_CHUNK
    tt = min(seq, HG_TILE)
    nt = seq // tt
    mcat, pairs, n_levels = _hgrn_masks(chunk, chunk)
    cq, _, ci, cg = _hgrn_col_blocks(cols)

    def col(cb):
        return pl.BlockSpec((tt, LANES), lambda b, h, t, cb=cb: (b * nt + t, cb + h))

    const2 = lambda shape: pl.BlockSpec(shape, lambda b, h, t: (0, 0))
    kern = functools.partial(_hgrn_prompt_kernel, n_levels=n_levels, chunk=chunk)
    return pl.pallas_call(
        kern,
        grid=(batch, n_heads_c, nt),
        in_specs=[col(cq), col(0), col(ci), col(cg),
                  pl.BlockSpec((1, LANES), lambda b, h, t: (0, h)),
                  const2((1, LANES)), const2(mcat.shape),
                  pl.BlockSpec(pairs.shape, lambda b, h, t: (0, 0, 0))],
        out_specs=[pl.BlockSpec((tt, LANES), lambda b, h, t: (b * nt + t, h)),
                   pl.BlockSpec((None, None, LANES, LANES), lambda b, h, t: (b, h, 0, 0))],
        out_shape=[jax.ShapeDtypeStruct((n, n_heads_c * LANES), BF16),
                   jax.ShapeDtypeStruct((batch, n_heads_c, LANES, LANES), F32)],
        scratch_shapes=[pltpu.VMEM((LANES, LANES), F32)],
        compiler_params=_params("parallel", "parallel", "arbitrary"),
        name="hgrn_prompt",
    )(y, yf, y, y, lower, norm_w, mcat, pairs)


def _hgrn_sample(ys, lower, norm_w, state, layer, t_new, n_heads_c, cols, prev):
    n = ys.shape[0]
    chunk = HG_CHUNK
    g = chunk // t_new
    mcat, pairs, n_levels = _hgrn_masks(chunk, t_new)
    cq, cf, ci, cg = _hgrn_col_blocks(cols)

    def col(cb):
        return pl.BlockSpec((chunk, LANES), lambda i, h, cb=cb: (i, cb + h))

    const2 = lambda shape: pl.BlockSpec(shape, lambda i, h: (0, 0))
    state_spec = pl.BlockSpec((None, g, None, LANES, LANES), lambda i, h: (layer, i, h, 0, 0))
    in_specs = [col(cq), col(cf), col(ci), col(cg),
                pl.BlockSpec((1, LANES), lambda i, h: (0, h)),
                const2((1, LANES)), const2(mcat.shape),
                pl.BlockSpec(pairs.shape, lambda i, h: (0, 0, 0)), state_spec,
                pl.BlockSpec(memory_space=pl.ANY)]
    args = [ys, ys, ys, ys, lower, norm_w, mcat, pairs, state, prev]
    aliases = {len(args) - 1: 1}
    kern = functools.partial(_hgrn_sample_kernel, n_levels=n_levels, seg=t_new)
    return pl.pallas_call(
        kern,
        grid=(n // chunk, n_heads_c),
        in_specs=in_specs,
        out_specs=[pl.BlockSpec((chunk, LANES), lambda i, h: (i, h)), state_spec],
        out_shape=[jax.ShapeDtypeStruct((n, n_heads_c * LANES), F32),
                   jax.ShapeDtypeStruct(state.shape, F32)],
        input_output_aliases=aliases,
        compiler_params=_params("parallel", "parallel"),
        name="hgrn_sample",
    )(*args)


def _conv_prompt(u, w, carry_ref):
    tm = u.shape[0]
    r = lax.broadcasted_iota(jnp.int32, (tm, 1), 0)
    c6 = carry_ref[SUBLANES - 2:SUBLANES - 1, :]
    c7 = carry_ref[SUBLANES - 1:SUBLANES, :]
    u1 = jnp.where(r == 0, c7, pltpu.roll(u, 1, 0))
    u2 = jnp.where(r == 0, c6, jnp.where(r == 1, c7, pltpu.roll(u, 2, 0)))
    carry_ref[...] = u[tm - SUBLANES:, :]
    return w[2:3, :] * u + w[1:2, :] * u1 + w[0:1, :] * u2


def _conv_sample(u, w, hist, t_new):
    tm = u.shape[0]
    pos = lax.broadcasted_iota(jnp.int32, (tm, 1), 0) % t_new
    u1 = jnp.where(pos == 0, pltpu.roll(hist, tm - 1, 0), pltpu.roll(u, 1, 0))
    u2 = jnp.where(pos < 2, hist, pltpu.roll(u, 2, 0))
    return w[2:3, :] * u + w[1:2, :] * u1 + w[0:1, :] * u2


def _layer_norm(h, g, b):
    hc = h - jnp.mean(h, axis=-1, keepdims=True)
    var = jnp.mean(hc * hc, axis=-1, keepdims=True)
    return hc * lax.rsqrt(var + LN_EPS) * g + b


def _merge_kernel(x_ref, gb_ref, gc_ref, hb_ref, ma_ref, mb_ref, mc_ref, ya_ref, yc_ref,
                  cw_ref, wa_ref, wb_ref, wc_ref, wo_ref, g_ref, b_ref, *rest,
                  sample, t_new, tiles_per_seq, alpha):
    if sample:
        hist_ref, x1_ref, x1b_ref, u_ref = rest
    else:
        x1_ref, x1b_ref, tail_ref, carry_ref = rest

        @pl.when(pl.program_id(0) % tiles_per_seq == 0)
        def _():
            carry_ref[...] = jnp.zeros_like(carry_ref)

    u = gc_ref[...].astype(F32) * hb_ref[...].astype(F32)
    cw = cw_ref[...]
    if sample:
        conv = _conv_sample(u, cw, hist_ref[...], t_new)
        u_ref[...] = u
    else:
        conv = _conv_prompt(u, cw, carry_ref)
        tail_ref[...] = u[u.shape[0] - SUBLANES:, :]
    yb = gb_ref[...].astype(F32) * conv
    gate = lambda ref: jax.nn.sigmoid(ref[...].astype(F32))
    mix = (gate(ma_ref) * _dot(ya_ref[...].astype(BF16), wa_ref[...])
           + gate(mb_ref) * _dot(yb.astype(BF16), wb_ref[...])
           + gate(mc_ref) * _dot(yc_ref[...].astype(BF16), wc_ref[...]))
    h = alpha * x_ref[...] + _dot(mix.astype(BF16), wo_ref[...])
    x1 = _layer_norm(h, g_ref[...], b_ref[...])
    x1_ref[...] = x1
    x1b_ref[...] = x1.astype(BF16)


def _merge(x, y, ya, yc, conv_w, wa, wb, wc, wo, ln_g, ln_b, alpha, cols, seq, hist=None,
           t_new=None):
    n, d = x.shape
    w_b = conv_w.shape[1]
    sample = hist is not None
    tm = min(n, MERGE_TM)
    nt = n // tm
    off_b = cols["gate_b"] // w_b
    off_m = cols["m_a"] // d
    row = lambda width, cb: pl.BlockSpec((tm, width), lambda i, cb=cb: (i, cb))
    const = lambda a: pl.BlockSpec(a.shape, lambda i: (0, 0))
    in_specs = [row(d, 0), row(w_b, off_b), row(w_b, off_b + 1), row(w_b, off_b + 2),
                row(d, off_m), row(d, off_m + 1), row(d, off_m + 2),
                row(ya.shape[1], 0), row(yc.shape[1], 0),
                const(conv_w), const(wa), const(wb), const(wc), const(wo), const(ln_g), const(ln_b)]
    args = [x, y, y, y, y, y, y, ya, yc, conv_w, wa, wb, wc, wo, ln_g, ln_b]
    out_specs = [row(d, 0), row(d, 0)]
    out_shape = [jax.ShapeDtypeStruct((n, d), F32), jax.ShapeDtypeStruct((n, d), BF16)]
    scratch = []
    if sample:
        in_specs.append(row(w_b, 0))
        args.append(hist)
        out_specs.append(row(w_b, 0))
        out_shape.append(jax.ShapeDtypeStruct((n, w_b), F32))
    else:
        out_specs.append(pl.BlockSpec((SUBLANES, w_b), lambda i: (i, 0)))
        out_shape.append(jax.ShapeDtypeStruct((nt * SUBLANES, w_b), F32))
        scratch.append(pltpu.VMEM((SUBLANES, w_b), F32))
    kern = functools.partial(_merge_kernel, sample=sample, t_new=t_new,
                             tiles_per_seq=max(seq // tm, 1), alpha=alpha)
    return pl.pallas_call(
        kern, grid=(nt,), in_specs=in_specs, out_specs=out_specs, out_shape=out_shape,
        scratch_shapes=scratch, compiler_params=_params("arbitrary"),
        name="merge_sample" if sample else "merge_prompt",
    )(*args)


def _ffn_kernel(xb_ref, x_ref, wu_ref, cw_ref, cb_ref, wd_ref, g_ref, b_ref, *rest,
                sample, t_new, tiles_per_seq, alpha):
    if sample:
        hist_ref, o_ref, ob_ref, a_ref = rest
    else:
        o_ref, ob_ref, tail_ref, carry_ref = rest
    d_ff = wd_ref.shape[0]
    if not sample:
        @pl.when(pl.program_id(0) % tiles_per_seq == 0)
        def _():
            carry_ref[...] = jnp.zeros_like(carry_ref)

    xb = xb_ref[...]
    y = jnp.zeros(x_ref.shape, F32)
    chunk = min(d_ff, FFN_CHUNK)
    for c in range(d_ff // chunk):
        lo, hi = c * chunk, (c + 1) * chunk
        a = _dot(xb, wu_ref[:, lo:hi])
        gate = _dot(xb, wu_ref[:, d_ff + lo:d_ff + hi])
        cw = cw_ref[:, lo:hi]
        if sample:
            conv = _conv_sample(a, cw, hist_ref[:, lo:hi], t_new)
            a_ref[:, lo:hi] = a
        else:
            conv = _conv_prompt(a, cw, carry_ref.at[:, lo:hi])
            tail_ref[:, lo:hi] = a[a.shape[0] - SUBLANES:, :]
        h = jax.nn.gelu(conv + cb_ref[:, lo:hi]) * gate
        y = y + _dot(h.astype(BF16), wd_ref[lo:hi, :])
    x2 = _layer_norm(alpha * x_ref[...] + y, g_ref[...], b_ref[...])
    o_ref[...] = x2
    ob_ref[...] = x2.astype(BF16)


def _ffn(xb, x, w_up, conv_w, conv_b, w_down, ln_g, ln_b, alpha, seq, hist=None, t_new=None):
    n, d = x.shape
    d_ff = w_down.shape[0]
    sample = hist is not None
    tm = min(n, FFN_TM // 2 if sample else FFN_TM)
    nt = n // tm
    row = lambda width: pl.BlockSpec((tm, width), lambda i: (i, 0))
    const = lambda a: pl.BlockSpec(a.shape, lambda i: (0, 0), pipeline_mode=pl.Buffered(1))
    in_specs = [row(d), row(d), const(w_up), const(conv_w), const(conv_b), const(w_down),
                const(ln_g), const(ln_b)]
    args = [xb, x, w_up, conv_w, conv_b, w_down, ln_g, ln_b]
    out_specs = [row(d), row(d)]
    out_shape = [jax.ShapeDtypeStruct((n, d), F32), jax.ShapeDtypeStruct((n, d), BF16)]
    scratch = []
    if sample:
        in_specs.append(row(d_ff))
        args.append(hist)
        out_specs.append(row(d_ff))
        out_shape.append(jax.ShapeDtypeStruct((n, d_ff), F32))
    else:
        out_specs.append(pl.BlockSpec((SUBLANES, d_ff), lambda i: (i, 0)))
        out_shape.append(jax.ShapeDtypeStruct((nt * SUBLANES, d_ff), F32))
        scratch.append(pltpu.VMEM((SUBLANES, d_ff), F32))
    kern = functools.partial(_ffn_kernel, sample=sample, t_new=t_new,
                             tiles_per_seq=max(seq // tm, 1), alpha=alpha)
    return pl.pallas_call(
        kern, grid=(nt,), in_specs=in_specs, out_specs=out_specs, out_shape=out_shape,
        scratch_shapes=scratch, compiler_params=_params("arbitrary"),
        name="ffn_sample" if sample else "ffn_prompt",
    )(*args)


def _pad_hist(state, t_new):
    n_seq, nb, c = state.shape
    return jnp.pad(state, ((0, 0), (0, t_new - nb), (0, 0))).reshape(n_seq * t_new, c)


def _tails(tail, batch, tiles_per_seq):
    c = tail.shape[1]
    t = tail.reshape(batch, tiles_per_seq, SUBLANES, c)
    return t[:, tiles_per_seq - 1, SUBLANES - 2:, :]


def kernel(x_prompt, x_sample, cache_k, cache_v, state_conv, state_hgrn, state_ffn_conv, page_table,
           w_in, sb_bias, conv_w, hgrn_lb, hgrn_norm_w, w_br_a, w_br_b, w_br_c, w_o, ln1_g, ln1_b,
           w_up, ffn_conv_w, ffn_conv_b, w_down, ln2_g, ln2_b):
    depth, d_model, d_in = w_in.shape
    batch, seq, _ = x_prompt.shape
    n_seq, t_new, _ = x_sample.shape
    n_pool, page, n_heads, d_head = cache_k.shape[1:]
    w_a = n_heads * d_head
    w_b = conv_w.shape[2]
    n_heads_c, dk_c, dv_c = state_hgrn.shape[2:]
    d_ff = w_down.shape[1]
    names = ("q_a", "k_a", "v_a", "gate_b", "gate_c", "h_b", "q_c", "f_c", "i_c", "og_c",
             "m_a", "m_b", "m_c")
    widths = (w_a, w_a, w_a, w_b, w_b, w_b, n_heads_c * dk_c, n_heads_c * dk_c, n_heads_c * dv_c,
              n_heads_c * dv_c, d_model, d_model, d_model)
    sizes = dict(zip(names, widths))
    assert sum(widths) == d_in and dk_c == LANES and dv_c == LANES and 2 * d_head == LANES
    cols_s = _col_starts(names, sizes)
    cols_p = _col_starts([k for k in names if k not in ("k_a", "v_a")], sizes)
    alpha = (2 * depth) ** 0.25

    lbs = jnp.cumsum(jax.nn.softmax(hgrn_lb.astype(F32), axis=0), axis=0)
    lower = lbs - lbs[0]

    ckt = jnp.transpose(cache_k, (0, 1, 3, 4, 2)).reshape(depth, n_pool, w_a, page)
    cvt = jnp.transpose(cache_v, (0, 1, 3, 4, 2)).reshape(depth, n_pool, w_a, page)
    bf = lambda a: a.astype(BF16)
    w_in_b, wa_b, wb_b, wc_b, wo_b, wu_b, wd_b = map(bf, (w_in, w_br_a, w_br_b, w_br_c, w_o, w_up, w_down))
    kv0, kv1 = cols_s["k_a"], cols_s["gate_b"]
    w_rest_b = jnp.concatenate([w_in_b[:, :, :kv0], w_in_b[:, :, kv1:]], axis=2)
    w_kvt_b = jnp.transpose(w_in_b[:, :, kv0:kv1], (0, 2, 1))

    hp = x_prompt.reshape(batch * seq, d_model)
    hs = x_sample.reshape(n_seq * t_new, d_model)
    hpb, hsb = bf(hp), bf(hs)
    outs = {k: [] for k in ("conv_p", "s_p", "ffn_p", "k_s", "v_s", "conv_s", "ffn_s")}
    kv_all = [jnp.zeros((depth, batch, w_a, seq), F32) for _ in range(2)]
    s_all = jnp.zeros(state_hgrn.shape, F32)
    for l in range(depth):
        row = lambda a: a[l].reshape(1, -1)
        yp, yf = _proj_in_mixed(hpb, w_rest_b[l], cols_p["f_c"], sizes["f_c"])
        kv_all = _proj_kv_t(w_kvt_b[l], hpb, batch, seq, l, kv_all)
        ys = _proj_in(hsb, w_in_b[l])
        ya_p = _sb_prompt(yp, *kv_all, l, sb_bias[l], batch, seq, n_heads, d_head)
        ya_s = _sb_sample(ys, ckt, cvt, page_table, sb_bias[l], l, n_heads, d_head, t_new)
        yc_p, s_p = _hgrn_prompt(yp, yf, row(lower), row(hgrn_norm_w), batch, seq, n_heads_c, cols_p)
        yc_s, s_all = _hgrn_sample(ys, row(lower), row(hgrn_norm_w), state_hgrn, l, t_new, n_heads_c,
                                   cols_s, s_all)
        margs = (conv_w[l], wa_b[l], wb_b[l], wc_b[l], wo_b[l], row(ln1_g), row(ln1_b), alpha)
        x1p, x1pb, tail_b = _merge(hp, yp, ya_p, yc_p, *margs, cols_p, seq)
        x1s, x1sb, u_s = _merge(hs, ys, ya_s, yc_s, *margs, cols_s, t_new,
                                hist=_pad_hist(state_conv[l], t_new), t_new=t_new)
        fargs = (wu_b[l], ffn_conv_w[l], row(ffn_conv_b), wd_b[l], row(ln2_g), row(ln2_b), alpha)
        hp, hpb, tail_f = _ffn(x1pb, x1p, *fargs, seq)
        hs, hsb, a_s = _ffn(x1sb, x1s, *fargs, t_new,
                            hist=_pad_hist(state_ffn_conv[l], t_new), t_new=t_new)

        outs["conv_p"].append(_tails(tail_b, batch, seq // min(batch * seq, MERGE_TM)))
        outs["s_p"].append(s_p)
        outs["ffn_p"].append(_tails(tail_f, batch, seq // min(batch * seq, FFN_TM)))
        outs["k_s"].append(ys[:, kv0:kv0 + w_a].reshape(n_seq, t_new, n_heads, d_head))
        outs["v_s"].append(ys[:, kv0 + w_a:kv1].reshape(n_seq, t_new, n_heads, d_head))
        outs["conv_s"].append(u_s.reshape(n_seq, t_new, w_b)[:, t_new - 2:])
        outs["ffn_s"].append(a_s.reshape(n_seq, t_new, d_ff)[:, t_new - 2:])

    def heads_t(a):
        return jnp.transpose(a.reshape(depth, batch, n_heads, d_head, seq), (0, 1, 4, 2, 3))

    st = {k: jnp.stack(v) for k, v in outs.items()}
    return (hp.reshape(batch, seq, d_model), hs.reshape(n_seq, t_new, d_model),
            heads_t(kv_all[0]), heads_t(kv_all[1]), st["conv_p"], st["s_p"], st["ffn_p"],
            st["k_s"], st["v_s"], st["conv_s"], s_all, st["ffn_s"])
```

```python
import functools
import math

import numpy as np
import jax
import jax.numpy as jnp
from jax import lax
from jax.experimental import pallas as pl
from jax.experimental.pallas import tpu as pltpu

F32 = jnp.float32
BF16 = jnp.bfloat16

LANES = 128
SUBLANES = 8
CONV_TAPS = 3
LN_EPS = 1e-5
RMS_EPS = 1e-6
LOG2E = math.log2(math.e)
HG_CHUNK = 128
HG_TILE = 4096
MERGE_TM = 512
FFN_TM = 512
SB_TQ = 512
FFN_CHUNK = 2816
VMEM_LIMIT = 56 * 1024 * 1024


def _params(*sem):
    return pltpu.CompilerParams(dimension_semantics=sem, vmem_limit_bytes=VMEM_LIMIT)


def _split2(x):
    hi = lax.bitcast_convert_type(
        lax.bitcast_convert_type(x, jnp.uint32) & jnp.uint32(0xFFFF0000), F32)
    return hi.astype(BF16), (x - hi).astype(BF16)


def _dot(a, b):
    return jnp.dot(a, b, preferred_element_type=F32)


def _dot_nt(a, b):
    return lax.dot_general(a, b, (((1,), (1,)), ((), ())), preferred_element_type=F32)


def _mm_kernel(x_ref, w_ref, o_ref):
    o_ref[...] = _dot(x_ref[...], w_ref[...])


def _proj_in(xb, wb):
    n, k = xb.shape
    m = wb.shape[1]
    tm = min(n, 1024)
    tn = 1024
    return pl.pallas_call(
        _mm_kernel,
        grid=(n // tm, m // tn),
        in_specs=[pl.BlockSpec((tm, k), lambda i, j: (i, 0)),
                  pl.BlockSpec((k, tn), lambda i, j: (0, j))],
        out_specs=pl.BlockSpec((tm, tn), lambda i, j: (i, j)),
        out_shape=jax.ShapeDtypeStruct((n, m), F32),
        compiler_params=_params("parallel", "arbitrary"),
        name="proj_in",
    )(xb, wb)


def _mm_mixed_kernel(x_ref, w_ref, o_ref, f_ref, *, jf, off):
    acc = _dot(x_ref[...], w_ref[...])
    o_ref[...] = acc.astype(BF16)

    @pl.when(pl.program_id(1) == jf)
    def _():
        f_ref[...] = acc[:, off:off + f_ref.shape[1]]


def _proj_in_mixed(xb, wb, f32_start, f32_width):
    n, k = xb.shape
    m = wb.shape[1]
    tm = min(n, 2048)
    tn = 1024
    jf, off = divmod(f32_start, tn)
    assert off + f32_width <= tn
    return pl.pallas_call(
        functools.partial(_mm_mixed_kernel, jf=jf, off=off),
        grid=(n // tm, m // tn),
        in_specs=[pl.BlockSpec((tm, k), lambda i, j: (i, 0)),
                  pl.BlockSpec((k, tn), lambda i, j: (0, j))],
        out_specs=[pl.BlockSpec((tm, tn), lambda i, j: (i, j)),
                   pl.BlockSpec((tm, f32_width), lambda i, j: (i, 0))],
        out_shape=[jax.ShapeDtypeStruct((n, m), BF16), jax.ShapeDtypeStruct((n, f32_width), F32)],
        compiler_params=_params("parallel", "arbitrary"),
        name="proj_in_mixed",
    )(xb, wb)


def _mm_nt_kernel(w_ref, x_ref, *rest):
    k_ref, v_ref = rest[-2:]
    kv = _dot_nt(w_ref[...], x_ref[...])
    half = k_ref.shape[0]
    k_ref[...] = kv[:half]
    v_ref[...] = kv[half:]


def _proj_kv_t(wt, xb, batch, seq, layer, prev):
    m, k = wt.shape
    w = m // 2
    tm = min(seq, 1024)
    nt = seq // tm
    out_spec = pl.BlockSpec((None, None, w, tm), lambda b, t: (layer, b, 0, t))
    out_shape = jax.ShapeDtypeStruct(prev[0].shape, F32)
    return pl.pallas_call(
        _mm_nt_kernel,
        grid=(batch, nt),
        in_specs=[pl.BlockSpec((m, k), lambda b, t: (0, 0)),
                  pl.BlockSpec((tm, k), lambda b, t: (b * nt + t, 0)),
                  pl.BlockSpec(memory_space=pl.ANY), pl.BlockSpec(memory_space=pl.ANY)],
        out_specs=[out_spec, out_spec],
        out_shape=[out_shape, out_shape],
        input_output_aliases={2: 0, 3: 1},
        compiler_params=_params("parallel", "arbitrary"),
        name="proj_kv_t",
    )(wt, xb, *prev)


def _softplus2(z):
    neg_abs = lax.bitcast_convert_type(
        lax.bitcast_convert_type(z, jnp.uint32) | jnp.uint32(0x80000000), F32)
    return jnp.maximum(z, 0.0) + jnp.log2(1.0 + jnp.exp2(neg_abs))


def _suffix_sums(sp, tri2):
    n = sp.shape[1]
    hi, lo = _split2(sp)
    c = _dot(jnp.concatenate([hi, lo], axis=1), tri2)
    return c[:, :n], c[:, n:]


def _tri_suffix2(n):
    j = np.arange(n)[:, None]
    s = np.arange(n)[None, :]
    ext = np.concatenate([(j >= s).astype(np.float32), np.ones((n, LANES), np.float32)], axis=1)
    return jnp.asarray(np.concatenate([ext, ext], axis=0), dtype=BF16)


def _sb_prompt_kernel(bias_ref, q_ref, kt_ref, vt_ref, tri_ref, o_ref, kb_ref, vb_ref, carry_ref,
                      acc_ref, *, tq, c2):
    hp = pl.program_id(1)
    qi = pl.program_id(2)
    n_blk = kb_ref.shape[0]
    sub = tri_ref.shape[0] // 2

    @pl.when(qi == 0)
    def _():
        for j in range(n_blk):
            kb_ref[j] = kt_ref[:, j * tq:(j + 1) * tq].astype(BF16)
            vb_ref[j] = vt_ref[:, j * tq:(j + 1) * tq].astype(BF16)

    lane = lax.broadcasted_iota(jnp.int32, (1, LANES), 1)
    half = LANES // 2
    q = q_ref[...].astype(F32) * c2
    qh = [jnp.where((lane >= h * half) & (lane < (h + 1) * half), q, 0.0).astype(BF16)
          for h in range(2)]
    bias2 = [bias_ref[2 * hp + h] * LOG2E for h in range(2)]
    tri2 = tri_ref[...]
    carry_ref[...] = jnp.zeros_like(carry_ref)
    acc_ref[...] = jnp.zeros_like(acc_ref)

    def logits(h, j):
        return _dot(qh[h], kb_ref[j]) + bias2[h]

    def sums_of(z, mask):
        sp = _softplus2(z)
        if mask is not None:
            sp = jnp.where(mask, sp, 0.0)
        return [_suffix_sums(sp[:, i * sub:(i + 1) * sub], tri2) for i in range(tq // sub)]

    def weigh(h, j, z, sums, mask):
        carry = carry_ref[h]
        parts = []
        for i in reversed(range(tq // sub)):
            c, tot = sums[i]
            parts.append(z[:, i * sub:(i + 1) * sub] - c - carry)
            carry = carry + tot
        carry_ref[h] = carry
        w = jnp.exp2(jnp.concatenate(parts[::-1], axis=1))
        if mask is not None:
            w = jnp.where(mask, w, 0.0)
        acc_ref[h] += _dot_nt(w.astype(BF16), vb_ref[j])

    causal = (lax.broadcasted_iota(jnp.int32, (tq, tq), 1)
              < lax.broadcasted_iota(jnp.int32, (tq, tq), 0))
    zs = [logits(h, qi) for h in range(2)]
    sums = [sums_of(z, causal) for z in zs]
    for h in range(2):
        weigh(h, qi, zs[h], sums[h], causal)

    def body(i, _):
        j = qi - 1 - i
        for h in range(2):
            z = logits(h, j)
            weigh(h, j, z, sums_of(z, None), None)
        return 0

    lax.fori_loop(0, qi, body, 0)
    o_ref[...] = jnp.where(lane < half, acc_ref[0], acc_ref[1]).astype(o_ref.dtype)


def _sb_prompt(q_src, kt, vt, layer, sb_bias, batch, seq, n_heads, d_head):
    n = q_src.shape[0]
    w_a = n_heads * d_head
    tq = SB_TQ
    nq = seq // tq
    npair = w_a // LANES
    tri2 = _tri_suffix2(LANES)
    kern = functools.partial(_sb_prompt_kernel, tq=tq, c2=LOG2E * d_head ** -0.5)
    return pl.pallas_call(
        kern,
        grid_spec=pltpu.PrefetchScalarGridSpec(
            num_scalar_prefetch=1,
            grid=(batch, npair, nq),
            in_specs=[
                pl.BlockSpec((tq, LANES), lambda b, hp, qi, bias: (b * nq + qi, hp)),
                pl.BlockSpec((None, None, LANES, seq), lambda b, hp, qi, bias: (layer, b, hp, 0)),
                pl.BlockSpec((None, None, LANES, seq), lambda b, hp, qi, bias: (layer, b, hp, 0)),
                pl.BlockSpec(tri2.shape, lambda b, hp, qi, bias: (0, 0)),
            ],
            out_specs=pl.BlockSpec((tq, LANES), lambda b, hp, qi, bias: (b * nq + qi, hp)),
            scratch_shapes=[pltpu.VMEM((nq, LANES, tq), BF16), pltpu.VMEM((nq, LANES, tq), BF16),
                            pltpu.VMEM((2, tq, LANES), F32), pltpu.VMEM((2, tq, LANES), F32)],
        ),
        out_shape=jax.ShapeDtypeStruct((n, w_a), BF16),
        compiler_params=_params("parallel", "parallel", "arbitrary"),
        name="sb_prompt",
    )(sb_bias, q_src, kt, vt, tri2)


def _sb_sample_kernel(pt_ref, q_ref, kn_ref, vn_ref, bias_ref, tri_ref, *rest,
                      n_pages, n_heads, d_head, t_new, c2):
    kpages = rest[:n_pages]
    vpages = rest[n_pages:2 * n_pages]
    o_ref = rest[2 * n_pages]
    rows = n_heads * t_new
    w_a = n_heads * d_head
    page = tri_ref.shape[0] // 2
    head_mask = (lax.broadcasted_iota(jnp.int32, (rows, w_a), 0) // t_new
                 == lax.broadcasted_iota(jnp.int32, (rows, w_a), 1) // d_head)
    bias2 = bias_ref[...] * LOG2E
    tri2 = tri_ref[...]

    q = q_ref[...] * c2
    qbd = jnp.where(head_mask, jnp.concatenate([q] * n_heads, axis=0), 0.0).astype(BF16)

    pad = jnp.zeros((page - t_new, w_a), F32)
    kn = jnp.concatenate([kn_ref[...], pad], axis=0).astype(BF16)
    vn = jnp.concatenate([vn_ref[...], pad], axis=0).astype(BF16)
    mask = (lax.broadcasted_iota(jnp.int32, (rows, page), 1)
            < lax.broadcasted_iota(jnp.int32, (rows, page), 0) % t_new)
    zs = [_dot_nt(qbd, kn) + bias2]
    zs += [_dot(qbd, kpages[r][...].astype(BF16)) + bias2 for r in range(n_pages)]
    sps = [jnp.where(mask, _softplus2(zs[0]), 0.0)] + [_softplus2(z) for z in zs[1:]]
    c_all = _dot(jnp.concatenate([jnp.concatenate(_split2(sp), axis=1) for sp in sps], axis=0), tri2)
    carry = jnp.zeros((rows, LANES), F32)
    acc = jnp.zeros((rows, w_a), F32)
    for r in range(n_pages + 1):
        c = c_all[r * rows:(r + 1) * rows]
        w = jnp.exp2(zs[r] - c[:, :page] - carry)
        carry = carry + c[:, page:]
        if r == 0:
            acc = acc + _dot(jnp.where(mask, w, 0.0).astype(BF16), vn)
        else:
            acc = acc + _dot_nt(w.astype(BF16), vpages[r - 1][...].astype(BF16))
    a = jnp.where(head_mask, acc, 0.0)
    out = a[0:t_new]
    for h in range(1, n_heads):
        out = out + a[h * t_new:(h + 1) * t_new]
    o_ref[...] = out


def _sb_sample(ys, ckt, cvt, page_table, sb_bias, layer, n_heads, d_head, t_new):
    n = ys.shape[0]
    n_seq, n_pages = page_table.shape
    page = ckt.shape[3]
    w_a = n_heads * d_head
    rows = n_heads * t_new
    bias_rep = jnp.broadcast_to(jnp.repeat(sb_bias.astype(F32), t_new)[:, None], (rows, LANES))
    tri2 = _tri_suffix2(page)

    def page_spec(r):
        return pl.BlockSpec((None, None, w_a, page),
                            lambda b, pt, r=r: (layer, pt[b, n_pages - 1 - r], 0, 0))

    kern = functools.partial(_sb_sample_kernel, n_pages=n_pages, n_heads=n_heads, d_head=d_head,
                             t_new=t_new, c2=LOG2E * d_head ** -0.5)
    return pl.pallas_call(
        kern,
        grid_spec=pltpu.PrefetchScalarGridSpec(
            num_scalar_prefetch=1,
            grid=(n_seq,),
            in_specs=[
                pl.BlockSpec((t_new, w_a), lambda b, pt: (b, 0)),
                pl.BlockSpec((t_new, w_a), lambda b, pt: (b, 1)),
                pl.BlockSpec((t_new, w_a), lambda b, pt: (b, 2)),
                pl.BlockSpec((rows, LANES), lambda b, pt: (0, 0)),
                pl.BlockSpec(tri2.shape, lambda b, pt: (0, 0)),
            ] + [page_spec(r) for r in range(n_pages)] * 2,
            out_specs=pl.BlockSpec((t_new, w_a), lambda b, pt: (b, 0)),
        ),
        out_shape=jax.ShapeDtypeStruct((n, w_a), F32),
        compiler_params=_params("arbitrary"),
        name="sb_sample",
    )(page_table, ys, ys, ys, bias_rep, tri2, *([ckt] * n_pages), *([cvt] * n_pages))


def _hgrn_masks(c, seg):
    t = np.arange(c)[:, None]
    j = np.arange(c)[None, :]
    same = (t // seg) == (j // seg)
    halves = []
    h = seg // 2
    while h >= 1:
        halves.append(h)
        h //= 2
    blocks = [same & (j <= t), same & (j > t)]
    pairs = []
    for h in halves:
        upper_t = ((t // h) % 2) == 1
        m_t = (t // h) * h
        m_next = (t // h + 1) * h
        blocks.append((upper_t & (j >= m_t) & (j <= t)) | (~upper_t & (j > t) & (j < m_next)))
        s = j
        pairs.append(upper_t & ((((s // h) % 2) == 0)) & ((t // (2 * h)) == (s // (2 * h))))
    pairs.append(t == j)
    m_all = np.concatenate(blocks, axis=0).astype(np.float32)
    m_cat = np.concatenate([m_all, m_all], axis=1)
    return (jnp.asarray(m_cat, dtype=BF16), jnp.asarray(np.stack(pairs).astype(np.float32)),
            len(halves))


def _hgrn_gates(zf, lb):
    log_sig = jnp.minimum(zf, 0.0) - jnp.log(1.0 + jnp.exp(-jnp.abs(zf)))
    a1 = jnp.log(lb)
    a2 = jnp.log(1.0 - lb) + log_sig
    g = jnp.maximum(a1, a2) + jnp.log(1.0 + jnp.exp(-jnp.abs(a1 - a2)))
    return g, (1.0 - lb) * jax.nn.sigmoid(-zf)


def _hgrn_exponents(gs, mcat):
    cols = [jnp.concatenate(_split2(g), axis=0) for g in gs]
    return _dot(mcat, cols[0] if len(cols) == 1 else jnp.concatenate(cols, axis=1))


def _hgrn_intra(q, k, v, blk, pairs, n_levels):
    vb = v.astype(BF16)
    scores = _dot_nt(q.astype(BF16), k.astype(BF16)) * pairs[n_levels]
    for i in range(n_levels):
        ex = jnp.exp(blk(2 + i))
        scores = scores + _dot_nt((q * ex).astype(BF16), (k * ex).astype(BF16)) * pairs[i]
    eb = jnp.exp(blk(0))
    kd = k * jnp.exp(blk(1))
    return _dot(scores.astype(BF16), vb), (q * eb).astype(BF16), eb.T, kd.T.astype(BF16), vb


def _hgrn_finish(o, og, norm_w):
    o = o * lax.rsqrt(jnp.mean(o * o, axis=-1, keepdims=True) + RMS_EPS) * norm_w
    return o * (og * jax.nn.sigmoid(og))


def _hgrn_prompt_kernel(q_ref, f_ref, v_ref, og_ref, lb_ref, nw_ref, mcat_ref, pairs_ref,
                        y_ref, s_out_ref, s_ref, *, n_levels, chunk):
    ti = pl.program_id(2)

    @pl.when(ti == 0)
    def _():
        s_ref[...] = jnp.zeros_like(s_ref)

    pairs = pairs_ref[...]
    n_chunks = q_ref.shape[0] // chunk
    rows = lambda a, r: a[r * chunk:(r + 1) * chunk]
    g, k = _hgrn_gates(f_ref[...], lb_ref[...])
    e = _hgrn_exponents([rows(g, r) for r in range(n_chunks)], mcat_ref[...])
    q = q_ref[...].astype(F32)
    v = v_ref[...].astype(F32)
    intra = [_hgrn_intra(rows(q, r), rows(k, r), rows(v, r),
                         lambda i, r=r: e[i * chunk:(i + 1) * chunk, r * LANES:(r + 1) * LANES],
                         pairs, n_levels) for r in range(n_chunks)]
    s = s_ref[...]
    for r, (o, qe, eb_t, kd_t, vb) in enumerate(intra):
        o = o + _dot(qe, s.astype(BF16))
        s = s * eb_t[:, chunk - 1:chunk] + _dot(kd_t, vb)
        sl = pl.ds(r * chunk, chunk)
        y_ref[sl, :] = _hgrn_finish(o, og_ref[sl, :].astype(F32), nw_ref[...]).astype(y_ref.dtype)
    s_ref[...] = s

    @pl.when(ti == pl.num_programs(2) - 1)
    def _():
        s_out_ref[...] = s


def _hgrn_sample_kernel(q_ref, f_ref, v_ref, og_ref, lb_ref, nw_ref, mcat_ref, pairs_ref,
                        s0_ref, *rest, n_levels, seg):
    y_ref, s_out_ref = rest[-2:]
    chunk = q_ref.shape[0]
    n_seq = chunk // seg
    lane_seq = lax.broadcasted_iota(jnp.int32, (1, chunk), 1) // seg
    row_seq = lax.broadcasted_iota(jnp.int32, (chunk, 1), 0) // seg
    g, k = _hgrn_gates(f_ref[...], lb_ref[...])
    e = _hgrn_exponents([g], mcat_ref[...])
    o, qe, eb_t, kd_t, vb = _hgrn_intra(q_ref[...], k, v_ref[...],
                                        lambda i: e[i * chunk:(i + 1) * chunk],
                                        pairs_ref[...], n_levels)
    for i in range(n_seq):
        s0 = s0_ref[i]
        o = o + _dot(jnp.where(row_seq == i, qe, jnp.zeros_like(qe)), s0.astype(BF16))
        last = (i + 1) * seg - 1
        kd_i = jnp.where(lane_seq == i, kd_t, jnp.zeros_like(kd_t))
        s_out_ref[i] = s0 * eb_t[:, last:last + 1] + _dot(kd_i, vb)
    y_ref[...] = _hgrn_finish(o, og_ref[...], nw_ref[...])


def _col_starts(names, sizes):
    starts, off = {}, 0
    for name in names:
        starts[name] = off
        off += sizes[name]
    return starts


def _hgrn_col_blocks(cols):
    return tuple(cols[k] // LANES for k in ("q_c", "f_c", "i_c", "og_c"))


def _hgrn_prompt(y, yf, lower, norm_w, batch, seq, n_heads_c, cols):
    n = y.shape[0]
    chunk = HG_CHUNK
    tt = min(seq, HG_TILE)
    nt = seq // tt
    mcat, pairs, n_levels = _hgrn_masks(chunk, chunk)
    cq, _, ci, cg = _hgrn_col_blocks(cols)

    def col(cb):
        return pl.BlockSpec((tt, LANES), lambda b, h, t, cb=cb: (b * nt + t, cb + h))

    const2 = lambda shape: pl.BlockSpec(shape, lambda b, h, t: (0, 0))
    kern = functools.partial(_hgrn_prompt_kernel, n_levels=n_levels, chunk=chunk)
    return pl.pallas_call(
        kern,
        grid=(batch, n_heads_c, nt),
        in_specs=[col(cq), col(0), col(ci), col(cg),
                  pl.BlockSpec((1, LANES), lambda b, h, t: (0, h)),
                  const2((1, LANES)), const2(mcat.shape),
                  pl.BlockSpec(pairs.shape, lambda b, h, t: (0, 0, 0))],
        out_specs=[pl.BlockSpec((tt, LANES), lambda b, h, t: (b * nt + t, h)),
                   pl.BlockSpec((None, None, LANES, LANES), lambda b, h, t: (b, h, 0, 0))],
        out_shape=[jax.ShapeDtypeStruct((n, n_heads_c * LANES), BF16),
                   jax.ShapeDtypeStruct((batch, n_heads_c, LANES, LANES), F32)],
        scratch_shapes=[pltpu.VMEM((LANES, LANES), F32)],
        compiler_params=_params("parallel", "parallel", "arbitrary"),
        name="hgrn_prompt",
    )(y, yf, y, y, lower, norm_w, mcat, pairs)


def _hgrn_sample(ys, lower, norm_w, state, layer, t_new, n_heads_c, cols, prev):
    n = ys.shape[0]
    chunk = HG_CHUNK
    g = chunk // t_new
    mcat, pairs, n_levels = _hgrn_masks(chunk, t_new)
    cq, cf, ci, cg = _hgrn_col_blocks(cols)

    def col(cb):
        return pl.BlockSpec((chunk, LANES), lambda i, h, cb=cb: (i, cb + h))

    const2 = lambda shape: pl.BlockSpec(shape, lambda i, h: (0, 0))
    state_spec = pl.BlockSpec((None, g, None, LANES, LANES), lambda i, h: (layer, i, h, 0, 0))
    in_specs = [col(cq), col(cf), col(ci), col(cg),
                pl.BlockSpec((1, LANES), lambda i, h: (0, h)),
                const2((1, LANES)), const2(mcat.shape),
                pl.BlockSpec(pairs.shape, lambda i, h: (0, 0, 0)), state_spec,
                pl.BlockSpec(memory_space=pl.ANY)]
    args = [ys, ys, ys, ys, lower, norm_w, mcat, pairs, state, prev]
    aliases = {len(args) - 1: 1}
    kern = functools.partial(_hgrn_sample_kernel, n_levels=n_levels, seg=t_new)
    return pl.pallas_call(
        kern,
        grid=(n // chunk, n_heads_c),
        in_specs=in_specs,
        out_specs=[pl.BlockSpec((chunk, LANES), lambda i, h: (i, h)), state_spec],
        out_shape=[jax.ShapeDtypeStruct((n, n_heads_c * LANES), F32),
                   jax.ShapeDtypeStruct(state.shape, F32)],
        input_output_aliases=aliases,
        compiler_params=_params("parallel", "parallel"),
        name="hgrn_sample",
    )(*args)


def _conv_prompt(u, w, carry_ref):
    tm = u.shape[0]
    r = lax.broadcasted_iota(jnp.int32, (tm, 1), 0)
    c6 = carry_ref[SUBLANES - 2:SUBLANES - 1, :]
    c7 = carry_ref[SUBLANES - 1:SUBLANES, :]
    u1 = jnp.where(r == 0, c7, pltpu.roll(u, 1, 0))
    u2 = jnp.where(r == 0, c6, jnp.where(r == 1, c7, pltpu.roll(u, 2, 0)))
    carry_ref[...] = u[tm - SUBLANES:, :]
    return w[2:3, :] * u + w[1:2, :] * u1 + w[0:1, :] * u2


def _conv_sample(u, w, hist, t_new):
    tm = u.shape[0]
    pos = lax.broadcasted_iota(jnp.int32, (tm, 1), 0) % t_new
    u1 = jnp.where(pos == 0, pltpu.roll(hist, tm - 1, 0), pltpu.roll(u, 1, 0))
    u2 = jnp.where(pos < 2, hist, pltpu.roll(u, 2, 0))
    return w[2:3, :] * u + w[1:2, :] * u1 + w[0:1, :] * u2


def _layer_norm(h, g, b):
    hc = h - jnp.mean(h, axis=-1, keepdims=True)
    var = jnp.mean(hc * hc, axis=-1, keepdims=True)
    return hc * lax.rsqrt(var + LN_EPS) * g + b


def _merge_kernel(x_ref, gb_ref, gc_ref, hb_ref, ma_ref, mb_ref, mc_ref, ya_ref, yc_ref,
                  cw_ref, wa_ref, wb_ref, wc_ref, wo_ref, g_ref, b_ref, *rest,
                  sample, t_new, tiles_per_seq, alpha):
    if sample:
        hist_ref, x1_ref, x1b_ref, u_ref = rest
    else:
        x1_ref, x1b_ref, tail_ref, carry_ref = rest

        @pl.when(pl.program_id(0) % tiles_per_seq == 0)
        def _():
            carry_ref[...] = jnp.zeros_like(carry_ref)

    u = gc_ref[...].astype(F32) * hb_ref[...].astype(F32)
    cw = cw_ref[...]
    if sample:
        conv = _conv_sample(u, cw, hist_ref[...], t_new)
        u_ref[...] = u
    else:
        conv = _conv_prompt(u, cw, carry_ref)
        tail_ref[...] = u[u.shape[0] - SUBLANES:, :]
    yb = gb_ref[...].astype(F32) * conv
    gate = lambda ref: jax.nn.sigmoid(ref[...].astype(F32))
    mix = (gate(ma_ref) * _dot(ya_ref[...].astype(BF16), wa_ref[...])
           + gate(mb_ref) * _dot(yb.astype(BF16), wb_ref[...])
           + gate(mc_ref) * _dot(yc_ref[...].astype(BF16), wc_ref[...]))
    h = alpha * x_ref[...] + _dot(mix.astype(BF16), wo_ref[...])
    x1 = _layer_norm(h, g_ref[...], b_ref[...])
    x1_ref[...] = x1
    x1b_ref[...] = x1.astype(BF16)


def _merge(x, y, ya, yc, conv_w, wa, wb, wc, wo, ln_g, ln_b, alpha, cols, seq, hist=None,
           t_new=None):
    n, d = x.shape
    w_b = conv_w.shape[1]
    sample = hist is not None
    tm = min(n, MERGE_TM)
    nt = n // tm
    off_b = cols["gate_b"] // w_b
    off_m = cols["m_a"] // d
    row = lambda width, cb: pl.BlockSpec((tm, width), lambda i, cb=cb: (i, cb))
    const = lambda a: pl.BlockSpec(a.shape, lambda i: (0, 0))
    in_specs = [row(d, 0), row(w_b, off_b), row(w_b, off_b + 1), row(w_b, off_b + 2),
                row(d, off_m), row(d, off_m + 1), row(d, off_m + 2),
                row(ya.shape[1], 0), row(yc.shape[1], 0),
                const(conv_w), const(wa), const(wb), const(wc), const(wo), const(ln_g), const(ln_b)]
    args = [x, y, y, y, y, y, y, ya, yc, conv_w, wa, wb, wc, wo, ln_g, ln_b]
    out_specs = [row(d, 0), row(d, 0)]
    out_shape = [jax.ShapeDtypeStruct((n, d), F32), jax.ShapeDtypeStruct((n, d), BF16)]
    scratch = []
    if sample:
        in_specs.append(row(w_b, 0))
        args.append(hist)
        out_specs.append(row(w_b, 0))
        out_shape.append(jax.ShapeDtypeStruct((n, w_b), F32))
    else:
        out_specs.append(pl.BlockSpec((SUBLANES, w_b), lambda i: (i, 0)))
        out_shape.append(jax.ShapeDtypeStruct((nt * SUBLANES, w_b), F32))
        scratch.append(pltpu.VMEM((SUBLANES, w_b), F32))
    kern = functools.partial(_merge_kernel, sample=sample, t_new=t_new,
                             tiles_per_seq=max(seq // tm, 1), alpha=alpha)
    return pl.pallas_call(
        kern, grid=(nt,), in_specs=in_specs, out_specs=out_specs, out_shape=out_shape,
        scratch_shapes=scratch, compiler_params=_params("arbitrary"),
        name="merge_sample" if sample else "merge_prompt",
    )(*args)


def _ffn_kernel(xb_ref, x_ref, wu_ref, cw_ref, cb_ref, wd_ref, g_ref, b_ref, *rest,
                sample, t_new, tiles_per_seq, alpha):
    if sample:
        hist_ref, o_ref, ob_ref, a_ref = rest
    else:
        o_ref, ob_ref, tail_ref, carry_ref = rest
    d_ff = wd_ref.shape[0]
    if not sample:
        @pl.when(pl.program_id(0) % tiles_per_seq == 0)
        def _():
            carry_ref[...] = jnp.zeros_like(carry_ref)

    xb = xb_ref[...]
    y = jnp.zeros(x_ref.shape, F32)
    chunk = min(d_ff, FFN_CHUNK)
    for c in range(d_ff // chunk):
        lo, hi = c * chunk, (c + 1) * chunk
        a = _dot(xb, wu_ref[:, lo:hi])
        gate = _dot(xb, wu_ref[:, d_ff + lo:d_ff + hi])
        cw = cw_ref[:, lo:hi]
        if sample:
            conv = _conv_sample(a, cw, hist_ref[:, lo:hi], t_new)
            a_ref[:, lo:hi] = a
        else:
            conv = _conv_prompt(a, cw, carry_ref.at[:, lo:hi])
            tail_ref[:, lo:hi] = a[a.shape[0] - SUBLANES:, :]
        h = jax.nn.gelu(conv + cb_ref[:, lo:hi]) * gate
        y = y + _dot(h.astype(BF16), wd_ref[lo:hi, :])
    x2 = _layer_norm(alpha * x_ref[...] + y, g_ref[...], b_ref[...])
    o_ref[...] = x2
    ob_ref[...] = x2.astype(BF16)


def _ffn(xb, x, w_up, conv_w, conv_b, w_down, ln_g, ln_b, alpha, seq, hist=None, t_new=None):
    n, d = x.shape
    d_ff = w_down.shape[0]
    sample = hist is not None
    tm = min(n, FFN_TM // 2 if sample else FFN_TM)
    nt = n // tm
    row = lambda width: pl.BlockSpec((tm, width), lambda i: (i, 0))
    const = lambda a: pl.BlockSpec(a.shape, lambda i: (0, 0), pipeline_mode=pl.Buffered(1))
    in_specs = [row(d), row(d), const(w_up), const(conv_w), const(conv_b), const(w_down),
                const(ln_g), const(ln_b)]
    args = [xb, x, w_up, conv_w, conv_b, w_down, ln_g, ln_b]
    out_specs = [row(d), row(d)]
    out_shape = [jax.ShapeDtypeStruct((n, d), F32), jax.ShapeDtypeStruct((n, d), BF16)]
    scratch = []
    if sample:
        in_specs.append(row(d_ff))
        args.append(hist)
        out_specs.append(row(d_ff))
        out_shape.append(jax.ShapeDtypeStruct((n, d_ff), F32))
    else:
        out_specs.append(pl.BlockSpec((SUBLANES, d_ff), lambda i: (i, 0)))
        out_shape.append(jax.ShapeDtypeStruct((nt * SUBLANES, d_ff), F32))
        scratch.append(pltpu.VMEM((SUBLANES, d_ff), F32))
    kern = functools.partial(_ffn_kernel, sample=sample, t_new=t_new,
                             tiles_per_seq=max(seq // tm, 1), alpha=alpha)
    return pl.pallas_call(
        kern, grid=(nt,), in_specs=in_specs, out_specs=out_specs, out_shape=out_shape,
        scratch_shapes=scratch, compiler_params=_params("arbitrary"),
        name="ffn_sample" if sample else "ffn_prompt",
    )(*args)


def _pad_hist(state, t_new):
    n_seq, nb, c = state.shape
    return jnp.pad(state, ((0, 0), (0, t_new - nb), (0, 0))).reshape(n_seq * t_new, c)


def _tails(tail, batch, tiles_per_seq):
    c = tail.shape[1]
    t = tail.reshape(batch, tiles_per_seq, SUBLANES, c)
    return t[:, tiles_per_seq - 1, SUBLANES - 2:, :]


def kernel(x_prompt, x_sample, cache_k, cache_v, state_conv, state_hgrn, state_ffn_conv, page_table,
           w_in, sb_bias, conv_w, hgrn_lb, hgrn_norm_w, w_br_a, w_br_b, w_br_c, w_o, ln1_g, ln1_b,
           w_up, ffn_conv_w, ffn_conv_b, w_down, ln2_g, ln2_b):
    depth, d_model, d_in = w_in.shape
    batch, seq, _ = x_prompt.shape
    n_seq, t_new, _ = x_sample.shape
    n_pool, page, n_heads, d_head = cache_k.shape[1:]
    w_a = n_heads * d_head
    w_b = conv_w.shape[2]
    n_heads_c, dk_c, dv_c = state_hgrn.shape[2:]
    d_ff = w_down.shape[1]
    names = ("q_a", "k_a", "v_a", "gate_b", "gate_c", "h_b", "q_c", "f_c", "i_c", "og_c",
             "m_a", "m_b", "m_c")
    widths = (w_a, w_a, w_a, w_b, w_b, w_b, n_heads_c * dk_c, n_heads_c * dk_c, n_heads_c * dv_c,
              n_heads_c * dv_c, d_model, d_model, d_model)
    sizes = dict(zip(names, widths))
    assert sum(widths) == d_in and dk_c == LANES and dv_c == LANES and 2 * d_head == LANES
    cols_s = _col_starts(names, sizes)
    cols_p = _col_starts([k for k in names if k not in ("k_a", "v_a")], sizes)
    alpha = (2 * depth) ** 0.25

    lbs = jnp.cumsum(jax.nn.softmax(hgrn_lb.astype(F32), axis=0), axis=0)
    lower = lbs - lbs[0]

    ckt = jnp.transpose(cache_k, (0, 1, 3, 4, 2)).reshape(depth, n_pool, w_a, page)
    cvt = jnp.transpose(cache_v, (0, 1, 3, 4, 2)).reshape(depth, n_pool, w_a, page)
    bf = lambda a: a.astype(BF16)
    w_in_b, wa_b, wb_b, wc_b, wo_b, wu_b, wd_b = map(bf, (w_in, w_br_a, w_br_b, w_br_c, w_o, w_up, w_down))
    kv0, kv1 = cols_s["k_a"], cols_s["gate_b"]
    w_rest_b = jnp.concatenate([w_in_b[:, :, :kv0], w_in_b[:, :, kv1:]], axis=2)
    w_kvt_b = jnp.transpose(w_in_b[:, :, kv0:kv1], (0, 2, 1))

    hp = x_prompt.reshape(batch * seq, d_model)
    hs = x_sample.reshape(n_seq * t_new, d_model)
    hpb, hsb = bf(hp), bf(hs)
    outs = {k: [] for k in ("conv_p", "s_p", "ffn_p", "k_s", "v_s", "conv_s", "ffn_s")}
    kv_all = [jnp.zeros((depth, batch, w_a, seq), F32) for _ in range(2)]
    s_all = jnp.zeros(state_hgrn.shape, F32)
    for l in range(depth):
        row = lambda a: a[l].reshape(1, -1)
        yp, yf = _proj_in_mixed(hpb, w_rest_b[l], cols_p["f_c"], sizes["f_c"])
        kv_all = _proj_kv_t(w_kvt_b[l], hpb, batch, seq, l, kv_all)
        ys = _proj_in(hsb, w_in_b[l])
        ya_p = _sb_prompt(yp, *kv_all, l, sb_bias[l], batch, seq, n_heads, d_head)
        ya_s = _sb_sample(ys, ckt, cvt, page_table, sb_bias[l], l, n_heads, d_head, t_new)
        yc_p, s_p = _hgrn_prompt(yp, yf, row(lower), row(hgrn_norm_w), batch, seq, n_heads_c, cols_p)
        yc_s, s_all = _hgrn_sample(ys, row(lower), row(hgrn_norm_w), state_hgrn, l, t_new, n_heads_c,
                                   cols_s, s_all)
        margs = (conv_w[l], wa_b[l], wb_b[l], wc_b[l], wo_b[l], row(ln1_g), row(ln1_b), alpha)
        x1p, x1pb, tail_b = _merge(hp, yp, ya_p, yc_p, *margs, cols_p, seq)
        x1s, x1sb, u_s = _merge(hs, ys, ya_s, yc_s, *margs, cols_s, t_new,
                                hist=_pad_hist(state_conv[l], t_new), t_new=t_new)
        fargs = (wu_b[l], ffn_conv_w[l], row(ffn_conv_b), wd_b[l], row(ln2_g), row(ln2_b), alpha)
        hp, hpb, tail_f = _ffn(x1pb, x1p, *fargs, seq)
        hs, hsb, a_s = _ffn(x1sb, x1s, *fargs, t_new,
                            hist=_pad_hist(state_ffn_conv[l], t_new), t_new=t_new)

        outs["conv_p"].append(_tails(tail_b, batch, seq // min(batch * seq, MERGE_TM)))
        outs["s_p"].append(s_p)
        outs["ffn_p"].append(_tails(tail_f, batch, seq // min(batch * seq, FFN_TM)))
        outs["k_s"].append(ys[:, kv0:kv0 + w_a].reshape(n_seq, t_new, n_heads, d_head))
        outs["v_s"].append(ys[:, kv0 + w_a:kv1].reshape(n_seq, t_new, n_heads, d_head))
        outs["conv_s"].append(u_s.reshape(n_seq, t_new, w_b)[:, t_new - 2:])
        outs["ffn_s"].append(a_s.reshape(n_seq, t_new, d_ff)[:, t_new - 2:])

    def heads_t(a):
        return jnp.transpose(a.reshape(depth, batch, n_heads, d_head, seq), (0, 1, 4, 2, 3))

    st = {k: jnp.stack(v) for k, v in outs.items()}
    return (hp.reshape(batch, seq, d_model), hs.reshape(n_seq, t_new, d_model),
            heads_t(kv_all[0]), heads_t(kv_all[1]), st["conv_p"], st["s_p"], st["ffn_p"],
            st["k_s"], st["v_s"], st["conv_s"], s_all, st["ffn_s"])
```
